```python
import jax, jax.numpy as jnp
from jax import lax
import numpy as np

D_MODEL = 1024
BATCH = 4
SEQ = 8192
DEPTH = 4

GRID_W = 64
QBLOCK = 128
NORM_EPS = 1e-6
ROPE_THETA = 10000.0
NEG_INF = -1e30

MLA_HEADS = 8
MLA_Q_RANK = 384
MLA_KV_RANK = 256
MLA_NOPE = 64
MLA_ROPE = 32
MLA_V = 64

DIL_PAIRS = ((128, 1), (512, 4), (2048, 16))
DIL_HALF = 64
DIL_SLOTS = 4
DIL_GROUPS = len(DIL_PAIRS)
DIL_HEADS = DIL_SLOTS * DIL_GROUPS
DIL_HEAD_DIM = 64

GQA_HEADS = 16
GQA_KV_HEADS = 4
GQA_HEAD_DIM = 64

FFN_HIDDEN = -(-8 * D_MODEL // (3 * 256)) * 256

IN_A = MLA_Q_RANK + MLA_KV_RANK + MLA_ROPE
IN_B = 3 * DIL_HEADS * DIL_HEAD_DIM
MIX_IN = IN_A + IN_B
MIX_OUT = MLA_HEADS * MLA_V + DIL_SLOTS * DIL_HEAD_DIM
N_EVEN = (DEPTH + 1) // 2
N_ODD = DEPTH // 2

kernel_name = "hybrid_mla_dilated_axial_gqa_encoder"


def rmsnorm(x, g):
    xf = x.astype(jnp.float32)
    y = xf * lax.rsqrt(jnp.mean(xf * xf, axis=-1, keepdims=True) + NORM_EPS)
    return (y * g.astype(jnp.float32)).astype(x.dtype)


def rope_angles(pos, dim):
    freqs = ROPE_THETA ** (-jnp.arange(0, dim, 2, dtype=jnp.float32) / dim)
    ang = pos.astype(jnp.float32)[:, None] * freqs[None, :]
    return jnp.cos(ang), jnp.sin(ang)


def apply_rope(x, cos, sin):
    xf = x.astype(jnp.float32)
    x1, x2 = jnp.split(xf, 2, axis=-1)
    return jnp.concatenate([x1 * cos - x2 * sin, x1 * sin + x2 * cos], axis=-1).astype(x.dtype)


def blocked_attention(q, k, v, scale):
    b, h, s, dk = q.shape
    g = k.shape[1]
    r = h // g
    nq = s // QBLOCK
    qb = q.reshape(b, g, r, nq, QBLOCK, dk).transpose(3, 0, 1, 2, 4, 5)

    def one_block(qblk):
        sc = jnp.einsum('bgrqd,bgkd->bgrqk', qblk, k, preferred_element_type=jnp.float32) * scale
        p = jax.nn.softmax(sc, axis=-1)
        return jnp.einsum('bgrqk,bgkd->bgrqd', p.astype(v.dtype), v)

    out = lax.map(one_block, qb)
    return out.transpose(1, 2, 3, 0, 4, 5).reshape(b, h, s, -1)


def mla_mixer(h_a, q_norm_g, kv_norm_g, w_uq, w_ukv, cos, sin):
    b, s, _ = h_a.shape
    cq, ckv, k_rope = jnp.split(h_a, [MLA_Q_RANK, MLA_Q_RANK + MLA_KV_RANK], axis=-1)
    cq = rmsnorm(cq, q_norm_g)
    ckv = rmsnorm(ckv, kv_norm_g)
    q = jnp.einsum('bsr,rhd->bhsd', cq, w_uq)
    kv = jnp.einsum('bsr,rhd->bhsd', ckv, w_ukv)
    q_nope, q_rope = q[..., :MLA_NOPE], q[..., MLA_NOPE:]
    k_nope, v = kv[..., :MLA_NOPE], kv[..., MLA_NOPE:]
    q_rope = apply_rope(q_rope, cos, sin)
    k_rope = apply_rope(k_rope, cos, sin)[:, None]
    k_rope = jnp.broadcast_to(k_rope, (b, MLA_HEADS, s, MLA_ROPE))
    qh = jnp.concatenate([q_nope, q_rope], axis=-1)
    kh = jnp.concatenate([k_nope, k_rope], axis=-1)
    o = blocked_attention(qh, kh, v, (MLA_NOPE + MLA_ROPE) ** -0.5)
    return o.transpose(0, 2, 1, 3).reshape(b, s, MLA_HEADS * MLA_V)


def dilated_group_attention(q, k, v, dilation, slopes):
    b, hg, s, dh = q.shape
    P = DIL_HALF
    L = s // dilation
    nb = -(-L // P)
    Lp = nb * P

    def to_residue(t):
        return t.reshape(b, hg, L, dilation, dh).transpose(0, 1, 3, 2, 4)

    qr, kr, vr = to_residue(q), to_residue(k), to_residue(v)
    qb = jnp.pad(qr, ((0, 0), (0, 0), (0, 0), (0, Lp - L), (0, 0))).reshape(b, hg, dilation, nb, P, dh)

    def band(t):
        tb = jnp.pad(t, ((0, 0), (0, 0), (0, 0), (P, Lp - L + P), (0, 0))).reshape(b, hg, dilation, nb + 2, P, dh)
        return jnp.concatenate([tb[:, :, :, :-2], tb[:, :, :, 1:-1], tb[:, :, :, 2:]], axis=4)

    kb, vb = band(kr), band(vr)
    sc = jnp.einsum('bhrnqd,bhrnkd->bhrnqk', qb, kb, preferred_element_type=jnp.float32) * dh ** -0.5

    i = jnp.arange(P)[:, None]
    c = jnp.arange(3 * P)[None, :]
    rel = c - P - i
    uk = jnp.arange(nb)[:, None, None] * P + c - P
    valid = (jnp.abs(rel) <= DIL_HALF)[None] & (uk >= 0) & (uk < L)
    dist = (dilation * jnp.abs(rel)).astype(jnp.float32)
    bias = -slopes.astype(jnp.float32)[:, None, None, None, None] * dist

    sc = jnp.where(valid, sc + bias, NEG_INF)
    m = jnp.max(sc, axis=-1, keepdims=True)
    e = jnp.exp(sc - m)
    den = jnp.sum(e, axis=-1, keepdims=True)
    o = jnp.einsum('bhrnqk,bhrnkd->bhrnqd', (e / den).astype(v.dtype), vb)
    lse = (m + jnp.log(den))[..., 0]

    o = o.reshape(b, hg, dilation, Lp, dh)[:, :, :, :L].transpose(0, 1, 3, 2, 4).reshape(b, hg, s, dh)
    lse = lse.reshape(b, hg, dilation, Lp)[:, :, :, :L].transpose(0, 1, 3, 2).reshape(b, hg, s)
    return o, lse


def dilated_mixer(h_b, slopes):
    b, s, _ = h_b.shape

    def heads(t):
        return t.reshape(b, s, DIL_GROUPS, DIL_SLOTS, DIL_HEAD_DIM).transpose(2, 0, 3, 1, 4)

    q, k, v = (heads(t) for t in jnp.split(h_b, 3, axis=-1))
    outs, lses = [], []
    for g, (_, dil) in enumerate(DIL_PAIRS):
        o, lse = dilated_group_attention(q[g], k[g], v[g], dil, slopes[g * DIL_SLOTS:(g + 1) * DIL_SLOTS])
        outs.append(o)
        lses.append(lse)
    outs = jnp.stack(outs, axis=0)
    wts = jax.nn.softmax(jnp.stack(lses, axis=0), axis=0)
    comb = jnp.sum(wts[..., None] * outs.astype(jnp.float32), axis=0).astype(h_b.dtype)
    return comb.transpose(0, 2, 1, 3).reshape(b, s, DIL_SLOTS * DIL_HEAD_DIM)


def gqa_axial_mixer(hn, w_q, w_kv, q_gain, k_gain, w_o, cos_r, sin_r, cos_c, sin_c):
    b, s, _ = hn.shape
    q = (hn @ w_q).reshape(b, s, GQA_HEADS, GQA_HEAD_DIM).transpose(0, 2, 1, 3)
    kv = (hn @ w_kv).reshape(b, s, 2, GQA_KV_HEADS, GQA_HEAD_DIM)
    k = kv[:, :, 0].transpose(0, 2, 1, 3)
    v = kv[:, :, 1].transpose(0, 2, 1, 3)
    q = rmsnorm(q, q_gain)
    k = rmsnorm(k, k_gain)
    half = GQA_HEAD_DIM // 2

    def axial(t):
        return jnp.concatenate([apply_rope(t[..., :half], cos_r, sin_r),
                                apply_rope(t[..., half:], cos_c, sin_c)], axis=-1)

    o = blocked_attention(axial(q), axial(k), v, GQA_HEAD_DIM ** -0.5)
    return o.transpose(0, 2, 1, 3).reshape(b, s, GQA_HEADS * GQA_HEAD_DIM) @ w_o


def swiglu(hn, w_in, w_out):
    gate, up = jnp.split(hn @ w_in, 2, axis=-1)
    return (jax.nn.silu(gate) * up) @ w_out


def setup_inputs(seed: int = 0) -> dict:
    key = jax.random.key(seed)
    ks = jax.random.split(key, 18)
    f32 = jnp.float32

    def w(k, shape, fan_in):
        return jax.random.normal(k, shape, f32) * fan_in ** -0.5

    def gain(k, shape):
        return 1.0 + 0.02 * jax.random.normal(k, shape, f32)

    ne, no = N_EVEN, N_ODD
    return {
        "x": jax.random.normal(ks[0], (BATCH, SEQ, D_MODEL), f32),
        "mix_norm_ab": gain(ks[1], (ne, D_MODEL)),
        "w_in_ab": w(ks[2], (ne, D_MODEL, MIX_IN), D_MODEL),
        "mla_q_norm": gain(ks[3], (ne, MLA_Q_RANK)),
        "mla_kv_norm": gain(ks[4], (ne, MLA_KV_RANK)),
        "mla_w_uq": w(ks[5], (ne, MLA_Q_RANK, MLA_HEADS, MLA_NOPE + MLA_ROPE), MLA_Q_RANK),
        "mla_w_ukv": w(ks[6], (ne, MLA_KV_RANK, MLA_HEADS, MLA_NOPE + MLA_V), MLA_KV_RANK),
        "w_out_ab": w(ks[7], (ne, MIX_OUT, D_MODEL), MIX_OUT),
        "mix_norm_c": gain(ks[8], (no, D_MODEL)),
        "gqa_w_q": w(ks[9], (no, D_MODEL, GQA_HEADS * GQA_HEAD_DIM), D_MODEL),
        "gqa_w_kv": w(ks[10], (no, D_MODEL, 2 * GQA_KV_HEADS * GQA_HEAD_DIM), D_MODEL),
        "gqa_q_norm": gain(ks[11], (no, GQA_HEAD_DIM)),
        "gqa_k_norm": gain(ks[12], (no, GQA_HEAD_DIM)),
        "gqa_w_o": w(ks[13], (no, GQA_HEADS * GQA_HEAD_DIM, D_MODEL), GQA_HEADS * GQA_HEAD_DIM),
        "ffn_norm": gain(ks[14], (DEPTH, D_MODEL)),
        "ffn_w_in": w(ks[15], (DEPTH, D_MODEL, 2 * FFN_HIDDEN), D_MODEL),
        "ffn_w_out": w(ks[16], (DEPTH, FFN_HIDDEN, D_MODEL), FFN_HIDDEN),
        "final_norm": gain(ks[17], (D_MODEL,)),
    }


def reference(x, mix_norm_ab, w_in_ab, mla_q_norm, mla_kv_norm, mla_w_uq, mla_w_ukv, w_out_ab,
              mix_norm_c, gqa_w_q, gqa_w_kv, gqa_q_norm, gqa_k_norm, gqa_w_o,
              ffn_norm, ffn_w_in, ffn_w_out, final_norm):
    s = x.shape[1]
    rows = s // GRID_W
    pos = jnp.arange(s)
    cos_t, sin_t = rope_angles(pos, MLA_ROPE)
    row_idx = jnp.broadcast_to(jnp.arange(rows)[:, None], (rows, GRID_W)).reshape(-1)
    col_idx = jnp.broadcast_to(jnp.arange(GRID_W)[None, :], (rows, GRID_W)).reshape(-1)
    cos_r, sin_r = rope_angles(row_idx, GQA_HEAD_DIM // 2)
    cos_c, sin_c = rope_angles(col_idx, GQA_HEAD_DIM // 2)
    slopes = jnp.exp2(-8.0 * jnp.arange(1, DIL_HEADS + 1, dtype=jnp.float32) / DIL_HEADS)

    for layer in range(DEPTH):
        i = layer // 2
        if layer % 2 == 0:
            z = rmsnorm(x, mix_norm_ab[i]) @ w_in_ab[i]
            o_a = mla_mixer(z[..., :IN_A], mla_q_norm[i], mla_kv_norm[i], mla_w_uq[i], mla_w_ukv[i],
                            cos_t, sin_t)
            o_b = dilated_mixer(z[..., IN_A:], slopes)
            x = x + jnp.concatenate([o_a, o_b], axis=-1) @ w_out_ab[i]
        else:
            x = x + gqa_axial_mixer(rmsnorm(x, mix_norm_c[i]), gqa_w_q[i], gqa_w_kv[i], gqa_q_norm[i],
                                    gqa_k_norm[i], gqa_w_o[i], cos_r, sin_r, cos_c, sin_c)
        x = x + swiglu(rmsnorm(x, ffn_norm[layer]), ffn_w_in[layer], ffn_w_out[layer])
    return rmsnorm(x, final_norm)
```

```python
import functools
import math

import jax
import jax.numpy as jnp
from jax import lax
from jax.experimental import pallas as pl
from jax.experimental.pallas import tpu as pltpu

F32 = jnp.float32
BF16 = jnp.bfloat16

NORM_EPS = 1e-6
ROPE_THETA = 10000.0
GRID_W = 64
LOG2E = math.log2(math.e)
LN2 = math.log(2.0)
MASKED = -1e30

LANES = 128
HEAD_DIM = 64

MLA_HEADS = 8
MLA_Q_RANK = 384
MLA_KV_RANK = 256
MLA_NOPE = 64
MLA_ROPE = 32

DIL_PAIRS = ((128, 1), (512, 4), (2048, 16))
DIL_HALF = 64
DIL_SLOTS = 4
DIL_GROUPS = 3
DIL_HEADS = DIL_SLOTS * DIL_GROUPS
DIL_WIDTH = DIL_SLOTS * HEAD_DIM

GQA_HEADS = 16
GQA_KV_HEADS = 4

VMEM_LIMIT = 56 * 1024 * 1024


def _cparams(sem):
    return pltpu.CompilerParams(dimension_semantics=sem, vmem_limit_bytes=VMEM_LIMIT)


def _const_spec(shape):
    nd = len(shape)
    return pl.BlockSpec(shape, lambda *_: (0,) * nd, pipeline_mode=pl.Buffered(1))


def _rmsnorm(x, g):
    ms = jnp.mean(x * x, axis=-1, keepdims=True)
    return x * lax.rsqrt(ms + NORM_EPS) * g


def _dot(a, b):
    return jnp.dot(a, b, preferred_element_type=F32)


def _rope_slab(t, cos_t, sin_t, first_half):
    partner = jnp.where(first_half, pltpu.roll(t, LANES - 16, 1), pltpu.roll(t, 16, 1))
    return t * cos_t + partner * sin_t


def _first_half_mask(rows):
    lane = lax.broadcasted_iota(jnp.int32, (rows, LANES), 1)
    return (lane % 32) < 16


def _even_in_kernel(x_ref, g_ref, wa_ref, wb_ref, gq_ref, gkv_ref, wuq_ref, wuk_ref, wuv_ref,
                    cos_ref, sin_ref, q_ref, k_ref, v_ref, zb_ref, *, q_scale, dil_scale):
    ts = x_ref.shape[1]
    xn = _rmsnorm(x_ref[0], g_ref[...]).astype(BF16)
    za = _dot(xn, wa_ref[...])
    zb = _dot(xn, wb_ref[...])
    nq = DIL_GROUPS * DIL_WIDTH
    zb_ref[0, :, :nq] = (zb[:, :nq] * dil_scale).astype(BF16)
    zb_ref[0, :, nq:] = zb[:, nq:].astype(BF16)

    cq = _rmsnorm(za[:, :MLA_Q_RANK], gq_ref[...]).astype(BF16)
    ckv = _rmsnorm(za[:, MLA_Q_RANK:MLA_Q_RANK + MLA_KV_RANK], gkv_ref[...]).astype(BF16)
    q_all = _dot(cq, wuq_ref[...])
    k_all = _dot(ckv, wuk_ref[...])
    v_all = _dot(ckv, wuv_ref[...])

    cos_t = cos_ref[...]
    sin_t = sin_ref[...]
    first = _first_half_mask(ts)
    lane = lax.broadcasted_iota(jnp.int32, (ts, LANES), 1)
    ones_col = (lane == HEAD_DIM).astype(F32)
    k_rope = _rope_slab(za[:, MLA_Q_RANK + MLA_KV_RANK:], cos_t, sin_t, first)
    for h in range(MLA_HEADS):
        sl = slice(h * LANES, (h + 1) * LANES)
        q_ref[0, h] = _rope_slab(q_all[:, sl] * q_scale, cos_t, sin_t, first).astype(BF16)
        k_ref[0, h] = (k_all[:, sl] + k_rope).astype(BF16)
        v_ref[0, h] = (v_all[:, sl] + ones_col).astype(BF16)


def _even_in_proj(x, g, wa, wb, gq, gkv, wuq, wuk, wuv, cos_t, sin_t, *, ts):
    b, s, d = x.shape
    n_s = s // ts
    head_out = jax.ShapeDtypeStruct((b, MLA_HEADS, s, LANES), BF16)
    head_spec = pl.BlockSpec((1, MLA_HEADS, ts, LANES), lambda bi, si: (bi, 0, si, 0))
    q_scale = (MLA_NOPE + MLA_ROPE) ** -0.5 * LOG2E
    dil_scale = HEAD_DIM ** -0.5 * LOG2E
    return pl.pallas_call(
        functools.partial(_even_in_kernel, q_scale=q_scale, dil_scale=dil_scale),
        out_shape=(head_out, head_out, head_out,
                   jax.ShapeDtypeStruct((b, s, wb.shape[1]), BF16)),
        grid=(b, n_s),
        in_specs=[
            pl.BlockSpec((1, ts, d), lambda bi, si: (bi, si, 0)),
            _const_spec(g.shape), _const_spec(wa.shape), _const_spec(wb.shape),
            _const_spec(gq.shape), _const_spec(gkv.shape),
            _const_spec(wuq.shape), _const_spec(wuk.shape), _const_spec(wuv.shape),
            pl.BlockSpec((ts, LANES), lambda bi, si: (si, 0)),
            pl.BlockSpec((ts, LANES), lambda bi, si: (si, 0)),
        ],
        out_specs=(head_spec, head_spec, head_spec,
                   pl.BlockSpec((1, ts, wb.shape[1]), lambda bi, si: (bi, si, 0))),
        compiler_params=_cparams(("parallel", "parallel")),
        name="even_in_proj",
    )(x, g, wa, wb, gq, gkv, wuq, wuk, wuv, cos_t, sin_t)


def _odd_in_kernel(x_ref, g_ref, wq_ref, wk_ref, wv_ref, gq_ref, gk_ref, cos_ref, sin_ref,
                   q_ref, k_ref, v_ref, *, q_scale):
    ts = x_ref.shape[1]
    xn = _rmsnorm(x_ref[0], g_ref[...]).astype(BF16)
    q_all = _dot(xn, wq_ref[...])
    k_all = _dot(xn, wk_ref[...])
    v_all = _dot(xn, wv_ref[...])
    cos_t = cos_ref[...]
    sin_t = sin_ref[...]
    first = _first_half_mask(ts)
    lane = lax.broadcasted_iota(jnp.int32, (ts, LANES), 1)
    ones_col = (lane == HEAD_DIM).astype(F32)

    def head_norm(t, gain):
        ms = jnp.sum(t * t, axis=-1, keepdims=True) * (1.0 / HEAD_DIM)
        return t * lax.rsqrt(ms + NORM_EPS) * gain

    gq = gq_ref[...] * q_scale
    gk = gk_ref[...]
    for h in range(GQA_HEADS):
        sl = slice(h * LANES, (h + 1) * LANES)
        q_ref[0, h] = _rope_slab(head_norm(q_all[:, sl], gq), cos_t, sin_t, first).astype(BF16)
    for h in range(GQA_KV_HEADS):
        sl = slice(h * LANES, (h + 1) * LANES)
        k_ref[0, h] = _rope_slab(head_norm(k_all[:, sl], gk), cos_t, sin_t, first).astype(BF16)
        v_ref[0, h] = (v_all[:, sl] + ones_col).astype(BF16)


def _odd_in_proj(x, g, wq, wk, wv, gq, gk, cos_t, sin_t, *, ts):
    b, s, d = x.shape
    n_s = s // ts
    q_scale = HEAD_DIM ** -0.5 * LOG2E
    return pl.pallas_call(
        functools.partial(_odd_in_kernel, q_scale=q_scale),
        out_shape=(jax.ShapeDtypeStruct((b, GQA_HEADS, s, LANES), BF16),
                   jax.ShapeDtypeStruct((b, GQA_KV_HEADS, s, LANES), BF16),
                   jax.ShapeDtypeStruct((b, GQA_KV_HEADS, s, LANES), BF16)),
        grid=(b, n_s),
        in_specs=[
            pl.BlockSpec((1, ts, d), lambda bi, si: (bi, si, 0)),
            _const_spec(g.shape), _const_spec(wq.shape), _const_spec(wk.shape),
            _const_spec(wv.shape), _const_spec(gq.shape), _const_spec(gk.shape),
            pl.BlockSpec((ts, LANES), lambda bi, si: (si, 0)),
            pl.BlockSpec((ts, LANES), lambda bi, si: (si, 0)),
        ],
        out_specs=(pl.BlockSpec((1, GQA_HEADS, ts, LANES), lambda bi, si: (bi, 0, si, 0)),
                   pl.BlockSpec((1, GQA_KV_HEADS, ts, LANES), lambda bi, si: (bi, 0, si, 0)),
                   pl.BlockSpec((1, GQA_KV_HEADS, ts, LANES), lambda bi, si: (bi, 0, si, 0))),
        compiler_params=_cparams(("parallel", "parallel")),
        name="odd_in_proj",
    )(x, g, wq, wk, wv, gq, gk, cos_t, sin_t)


def _attn_kernel(q_ref, k_ref, v_ref, o_ref, m_scr, acc_scr, *, hb, rep, tk):
    tq = q_ref.shape[2]
    s_len = k_ref.shape[2]
    m_rows = rep * tq
    outs = []
    for h in range(hb):
        q2 = q_ref[0, h * rep:(h + 1) * rep].reshape(m_rows, LANES)
        m_scr[...] = jnp.full((m_rows, 1), MASKED, F32)
        acc_scr[...] = jnp.zeros((m_rows, LANES), F32)

        def body(j, carry, h=h, q2=q2):
            start = pl.multiple_of(j * tk, tk)
            kc = k_ref[0, h, pl.ds(start, tk), :]
            vc = v_ref[0, h, pl.ds(start, tk), :]
            s = lax.dot_general(q2, kc, (((1,), (1,)), ((), ())), preferred_element_type=F32)
            m_old = m_scr[...]
            m_new = jnp.maximum(m_old, jnp.max(s, axis=-1, keepdims=True))
            p = jnp.exp2(s - m_new)
            alpha = jnp.exp2(m_old - m_new)
            acc_scr[...] = alpha * acc_scr[...] + _dot(p.astype(BF16), vc)
            m_scr[...] = m_new
            return carry

        lax.fori_loop(0, s_len // tk, body, 0)
        acc = acc_scr[...]
        o = acc[:, :HEAD_DIM] / acc[:, HEAD_DIM:HEAD_DIM + 1]
        for r in range(rep):
            outs.append(o[r * tq:(r + 1) * tq])
    o_ref[0] = jnp.concatenate(outs, axis=-1).astype(o_ref.dtype)


def _attention(q, k, v, *, hb, tq, tk):
    b, hq, s, _ = q.shape
    hk = k.shape[1]
    rep = hq // hk
    width = hb * rep * HEAD_DIM
    m_rows = rep * tq
    return pl.pallas_call(
        functools.partial(_attn_kernel, hb=hb, rep=rep, tk=tk),
        out_shape=jax.ShapeDtypeStruct((b, s, hq * HEAD_DIM), BF16),
        grid=(b, hk // hb, s // tq),
        in_specs=[
            pl.BlockSpec((1, hb * rep, tq, LANES), lambda bi, gi, qi: (bi, gi, qi, 0)),
            pl.BlockSpec((1, hb, s, LANES), lambda bi, gi, qi: (bi, gi, 0, 0)),
            pl.BlockSpec((1, hb, s, LANES), lambda bi, gi, qi: (bi, gi, 0, 0)),
        ],
        out_specs=pl.BlockSpec((1, tq, width), lambda bi, gi, qi: (bi, qi, gi)),
        scratch_shapes=[pltpu.VMEM((m_rows, 1), F32), pltpu.VMEM((m_rows, LANES), F32)],
        compiler_params=_cparams(("parallel", "parallel", "arbitrary")),
        name="dense_attention",
    )(q, k, v)


def _dilated_kernel(q_ref, k_ref, v_ref, o_ref, lse_ref, *, dilation, slopes, qb, win):
    length = k_ref.shape[1]
    u0 = pl.program_id(2) * qb
    start = jnp.clip(u0 - DIL_HALF, 0, length - win)
    start = pl.multiple_of(start, DIL_HALF)
    kw = k_ref[0, pl.ds(start, win), :]
    vw = v_ref[0, pl.ds(start, win), :]
    q = q_ref[0]
    row = lax.broadcasted_iota(jnp.int32, (qb, win), 0)
    col = lax.broadcasted_iota(jnp.int32, (qb, win), 1)
    rel = col - row + (start - u0)
    dist = jnp.abs(rel)
    valid = dist <= DIL_HALF
    distf = dist.astype(F32)
    lane = lax.broadcasted_iota(jnp.int32, (qb, DIL_WIDTH), 1)
    o_all = jnp.zeros((qb, DIL_WIDTH), F32)
    lse_all = jnp.zeros((qb, DIL_WIDTH), F32)
    for sl in range(DIL_SLOTS):
        in_slot = (lane >= sl * HEAD_DIM) & (lane < (sl + 1) * HEAD_DIM)
        q_s = jnp.where(in_slot, q, jnp.zeros_like(q))
        s = lax.dot_general(q_s, kw, (((1,), (1,)), ((), ())), preferred_element_type=F32)
        s = jnp.where(valid, s - (slopes[sl] * dilation * LOG2E) * distf, MASKED)
        m = jnp.max(s, axis=-1, keepdims=True)
        e = jnp.exp2(s - m)
        den = jnp.sum(e, axis=-1, keepdims=True)
        o = _dot((e / den).astype(BF16), vw)
        lse = (m + jnp.log2(den)) * LN2
        o_all = jnp.where(in_slot, o, o_all)
        lse_all = jnp.where(in_slot, lse, lse_all)
    o_ref[0] = o_all.astype(o_ref.dtype)
    lse_ref[0] = lse_all


def _dilated_group(zb, group, *, qb_max):
    b, s, cols = zb.shape
    _, dilation = DIL_PAIRS[group]
    length = s // dilation
    qb = min(qb_max, length)
    win = min(length, qb + 2 * DIL_HALF)
    nblk = cols // DIL_WIDTH
    view = zb.reshape(b, length, dilation * cols)
    slopes = tuple(2.0 ** (-8.0 * (group * DIL_SLOTS + i + 1) / DIL_HEADS) for i in range(DIL_SLOTS))
    o, lse = pl.pallas_call(
        functools.partial(_dilated_kernel, dilation=dilation, slopes=slopes, qb=qb, win=win),
        out_shape=(jax.ShapeDtypeStruct((b, length, dilation * DIL_WIDTH), BF16),
                   jax.ShapeDtypeStruct((b, length, dilation * DIL_WIDTH), F32)),
        grid=(b, dilation, length // qb),
        in_specs=[
            pl.BlockSpec((1, qb, DIL_WIDTH), lambda bi, r, j: (bi, j, r * nblk + group)),
            pl.BlockSpec((1, length, DIL_WIDTH), lambda bi, r, j: (bi, 0, r * nblk + 3 + group)),
            pl.BlockSpec((1, length, DIL_WIDTH), lambda bi, r, j: (bi, 0, r * nblk + 6 + group)),
        ],
        out_specs=(pl.BlockSpec((1, qb, DIL_WIDTH), lambda bi, r, j: (bi, j, r)),
                   pl.BlockSpec((1, qb, DIL_WIDTH), lambda bi, r, j: (bi, j, r))),
        compiler_params=_cparams(("parallel", "parallel", "arbitrary")),
        name=f"dilated_attention_g{group}",
    )(view, view, view)
    return o.reshape(b, s, DIL_WIDTH), lse.reshape(b, s, DIL_WIDTH)


def _ffn_tail(x1, gf_ref, wg_ref, wu_ref, wd_ref, gfin_ref, o_ref, *, chunks, final):
    hn = _rmsnorm(x1, gf_ref[...]).astype(BF16)
    hidden = wg_ref.shape[1]
    hc = hidden // chunks
    down = None
    for c in range(chunks):
        sl = slice(c * hc, (c + 1) * hc)
        gate = _dot(hn, wg_ref[:, sl])
        up = _dot(hn, wu_ref[:, sl])
        act = gate * (1.0 / (1.0 + jnp.exp(-gate))) * up
        part = _dot(act.astype(BF16), wd_ref[sl, :])
        down = part if down is None else down + part
    y = x1 + down
    if final:
        y = _rmsnorm(y, gfin_ref[...])
    o_ref[0] = y


def _even_out_kernel(x_ref, oa_ref, o0_ref, o1_ref, o2_ref, l0_ref, l1_ref, l2_ref,
                     woa_ref, wob_ref, gf_ref, wg_ref, wu_ref, wd_ref, gfin_ref, o_ref,
                     *, chunks, final):
    l0, l1, l2 = l0_ref[0], l1_ref[0], l2_ref[0]
    mx = jnp.maximum(jnp.maximum(l0, l1), l2)
    e0, e1, e2 = jnp.exp(l0 - mx), jnp.exp(l1 - mx), jnp.exp(l2 - mx)
    tot = e0 + e1 + e2
    comb = (e0 / tot) * o0_ref[0].astype(F32) + (e1 / tot) * o1_ref[0].astype(F32) \
        + (e2 / tot) * o2_ref[0].astype(F32)
    x1 = x_ref[0] + (_dot(oa_ref[0], woa_ref[...]) + _dot(comb.astype(BF16), wob_ref[...]))
    _ffn_tail(x1, gf_ref, wg_ref, wu_ref, wd_ref, gfin_ref, o_ref, chunks=chunks, final=final)


def _odd_out_kernel(x_ref, o_in_ref, wo_ref, gf_ref, wg_ref, wu_ref, wd_ref, gfin_ref, o_ref,
                    *, chunks, final):
    x1 = x_ref[0] + _dot(o_in_ref[0], wo_ref[...])
    _ffn_tail(x1, gf_ref, wg_ref, wu_ref, wd_ref, gfin_ref, o_ref, chunks=chunks, final=final)


def _out_ffn(kernel_fn, x, acts, consts, *, ts, chunks, final):
    b, s, d = x.shape

    def row_spec(a):
        return pl.BlockSpec((1, ts, a.shape[-1]), lambda bi, si: (bi, si, 0))

    return pl.pallas_call(
        functools.partial(kernel_fn, chunks=chunks, final=final),
        out_shape=jax.ShapeDtypeStruct((b, s, d), F32),
        grid=(b, s // ts),
        in_specs=[row_spec(x)] + [row_spec(a) for a in acts] + [_const_spec(c.shape) for c in consts],
        out_specs=row_spec(x),
        compiler_params=_cparams(("parallel", "parallel")),
        name=kernel_fn.__name__.strip("_"),
    )(x, *acts, *consts)


def _rope_angles(pos, dim):
    freqs = ROPE_THETA ** (-jnp.arange(0, dim, 2, dtype=F32) / dim)
    ang = pos.astype(F32)[:, None] * freqs[None, :]
    return jnp.cos(ang), jnp.sin(ang)


def _mla_tables(s):
    cos, sin = _rope_angles(jnp.arange(s), MLA_ROPE)
    one = jnp.ones((s, MLA_NOPE), F32)
    tail = LANES - MLA_NOPE - MLA_ROPE
    cos_t = jnp.concatenate([one, cos, cos, jnp.ones((s, tail), F32)], axis=-1)
    sin_t = jnp.concatenate([jnp.zeros((s, MLA_NOPE), F32), -sin, sin, jnp.zeros((s, tail), F32)], axis=-1)
    return cos_t, sin_t


def _axial_tables(s):
    rows = s // GRID_W
    row_idx = jnp.broadcast_to(jnp.arange(rows)[:, None], (rows, GRID_W)).reshape(-1)
    col_idx = jnp.broadcast_to(jnp.arange(GRID_W)[None, :], (rows, GRID_W)).reshape(-1)
    cos_r, sin_r = _rope_angles(row_idx, HEAD_DIM // 2)
    cos_c, sin_c = _rope_angles(col_idx, HEAD_DIM // 2)
    pad = LANES - HEAD_DIM
    cos_t = jnp.concatenate([cos_r, cos_r, cos_c, cos_c, jnp.ones((s, pad), F32)], axis=-1)
    sin_t = jnp.concatenate([-sin_r, sin_r, -sin_c, sin_c, jnp.zeros((s, pad), F32)], axis=-1)
    return cos_t, sin_t


def _head_slabs(w, fill):
    kdim, heads, d = w.shape
    return jnp.pad(w, ((0, 0), (0, 0), (0, fill - d))).reshape(kdim, heads * fill).astype(BF16)


def _row(v, width=None):
    v = v.reshape(1, -1).astype(F32)
    if width is not None:
        v = jnp.pad(v, ((0, 0), (0, width - v.shape[1])))
    return v


def kernel(x, mix_norm_ab, w_in_ab, mla_q_norm, mla_kv_norm, mla_w_uq, mla_w_ukv, w_out_ab,
           mix_norm_c, gqa_w_q, gqa_w_kv, gqa_q_norm, gqa_k_norm, gqa_w_o,
           ffn_norm, ffn_w_in, ffn_w_out, final_norm):
    b, s, d = x.shape
    depth = ffn_norm.shape[0]
    hidden = ffn_w_out.shape[1]
    ts = min(512, s)
    in_a = MLA_Q_RANK + MLA_KV_RANK
    mla_cos, mla_sin = _mla_tables(s)
    ax_cos, ax_sin = _axial_tables(s)
    gfin = _row(final_norm)

    for layer in range(depth):
        i = layer // 2
        final = layer == depth - 1
        gf = _row(ffn_norm[layer])
        wg = ffn_w_in[layer][:, :hidden].astype(BF16)
        wu = ffn_w_in[layer][:, hidden:].astype(BF16)
        wd = ffn_w_out[layer].astype(BF16)
        if layer % 2 == 0:
            w_in = w_in_ab[i]
            wa = jnp.concatenate([
                w_in[:, :in_a], jnp.zeros((d, MLA_NOPE), F32), w_in[:, in_a:in_a + MLA_ROPE],
                jnp.zeros((d, LANES - MLA_NOPE - MLA_ROPE), F32)], axis=-1).astype(BF16)
            wb = w_in[:, in_a + MLA_ROPE:].astype(BF16)
            wuq = _head_slabs(mla_w_uq[i], LANES)
            wuk = _head_slabs(mla_w_ukv[i][:, :, :MLA_NOPE], LANES)
            wuv = _head_slabs(mla_w_ukv[i][:, :, MLA_NOPE:], LANES)
            q, k, v, zb = _even_in_proj(
                x, _row(mix_norm_ab[i]), wa, wb, _row(mla_q_norm[i]), _row(mla_kv_norm[i]),
                wuq, wuk, wuv, mla_cos, mla_sin, ts=ts)
            o_a = _attention(q, k, v, hb=2, tq=min(512, s), tk=min(512, s))
            outs, lses = zip(*[_dilated_group(zb, g, qb_max=256) for g in range(DIL_GROUPS)])
            n_a = MLA_HEADS * HEAD_DIM
            x = _out_ffn(_even_out_kernel, x, (o_a,) + outs + lses,
                         (w_out_ab[i][:n_a].astype(BF16), w_out_ab[i][n_a:].astype(BF16),
                          gf, wg, wu, wd, gfin), ts=ts, chunks=2, final=final)
        else:
            wq = _head_slabs(gqa_w_q[i].reshape(d, GQA_HEADS, HEAD_DIM), LANES)
            wkv = gqa_w_kv[i].reshape(d, 2, GQA_KV_HEADS, HEAD_DIM)
            wk = _head_slabs(wkv[:, 0], LANES)
            wv = _head_slabs(wkv[:, 1], LANES)
            q, k, v = _odd_in_proj(
                x, _row(mix_norm_c[i]), wq, wk, wv, _row(gqa_q_norm[i], LANES),
                _row(gqa_k_norm[i], LANES), ax_cos, ax_sin, ts=ts)
            o = _attention(q, k, v, hb=1, tq=min(128, s), tk=min(512, s))
            x = _out_ffn(_odd_out_kernel, x, (o,),
                         (gqa_w_o[i].astype(BF16), gf, wg, wu, wd, gfin),
                         ts=ts, chunks=2, final=final)
    return x
```

```python
import functools
import math

import jax
import jax.numpy as jnp
from jax import lax
from jax.experimental import pallas as pl
from jax.experimental.pallas import tpu as pltpu

F32 = jnp.float32
BF16 = jnp.bfloat16

NORM_EPS = 1e-6
ROPE_THETA = 10000.0
GRID_W = 64
LOG2E = math.log2(math.e)
LN2 = math.log(2.0)
MASKED = -1e30

LANES = 128
HEAD_DIM = 64

MLA_HEADS = 8
MLA_Q_RANK = 384
MLA_KV_RANK = 256
MLA_NOPE = 64
MLA_ROPE = 32

DIL_PAIRS = ((128, 1), (512, 4), (2048, 16))
DIL_HALF = 64
DIL_SLOTS = 4
DIL_GROUPS = 3
DIL_HEADS = DIL_SLOTS * DIL_GROUPS
DIL_WIDTH = DIL_SLOTS * HEAD_DIM

GQA_HEADS = 16
GQA_KV_HEADS = 4

MLA_BIAS_LANE = MLA_NOPE + MLA_ROPE
GQA_BIAS_LANE = HEAD_DIM

FAST_BOUND_LIMIT = 60.0
BOUND_SLACK = 1.02

VMEM_LIMIT = 56 * 1024 * 1024


def _cparams(sem):
    return pltpu.CompilerParams(dimension_semantics=sem, vmem_limit_bytes=VMEM_LIMIT)


def _const_spec(shape):
    nd = len(shape)
    return pl.BlockSpec(shape, lambda *_: (0,) * nd, pipeline_mode=pl.Buffered(1))


def _rmsnorm(x, g):
    ms = jnp.mean(x * x, axis=-1, keepdims=True)
    return x * lax.rsqrt(ms + NORM_EPS) * g


def _dot(a, b):
    return jnp.dot(a, b, preferred_element_type=F32)


def _rope_slab(t, cos_t, sin_t, first_half):
    partner = jnp.where(first_half, pltpu.roll(t, LANES - 16, 1), pltpu.roll(t, 16, 1))
    return t * cos_t + partner * sin_t


def _first_half_mask(rows):
    lane = lax.broadcasted_iota(jnp.int32, (rows, LANES), 1)
    return (lane % 32) < 16


def _even_in_kernel(x_ref, g_ref, wa_ref, wb_ref, gq_ref, gkv_ref, wuq_ref, wuk_ref, wuv_ref,
                    cos_ref, sin_ref, q_ref, k_ref, v_ref, zb_ref, *, q_scale, dil_scale):
    ts = x_ref.shape[1]
    xn = _rmsnorm(x_ref[0], g_ref[...]).astype(BF16)
    za = _dot(xn, wa_ref[...])
    zb = _dot(xn, wb_ref[...])
    nq = DIL_GROUPS * DIL_WIDTH
    zb_ref[0, :, :nq] = (zb[:, :nq] * dil_scale).astype(BF16)
    zb_ref[0, :, nq:] = zb[:, nq:].astype(BF16)

    cq = _rmsnorm(za[:, :MLA_Q_RANK], gq_ref[...]).astype(BF16)
    ckv = _rmsnorm(za[:, MLA_Q_RANK:MLA_Q_RANK + MLA_KV_RANK], gkv_ref[...]).astype(BF16)
    q_all = _dot(cq, wuq_ref[...])
    k_all = _dot(ckv, wuk_ref[...])
    v_all = _dot(ckv, wuv_ref[...])

    cos_t = cos_ref[...]
    sin_t = sin_ref[...]
    first = _first_half_mask(ts)
    lane = lax.broadcasted_iota(jnp.int32, (ts, LANES), 1)
    ones_col = (lane == HEAD_DIM).astype(F32)
    k_rope = _rope_slab(za[:, MLA_Q_RANK + MLA_KV_RANK:], cos_t, sin_t, first) \
        + (lane == MLA_BIAS_LANE).astype(F32)
    for h in range(MLA_HEADS):
        sl = slice(h * LANES, (h + 1) * LANES)
        q_ref[0, h] = _rope_slab(q_all[:, sl] * q_scale, cos_t, sin_t, first).astype(BF16)
        k_ref[0, h] = (k_all[:, sl] + k_rope).astype(BF16)
        v_ref[0, h] = (v_all[:, sl] + ones_col).astype(BF16)


def _even_in_proj(x, g, wa, wb, gq, gkv, wuq, wuk, wuv, cos_t, sin_t, *, ts):
    b, s, d = x.shape
    n_s = s // ts
    head_out = jax.ShapeDtypeStruct((b, MLA_HEADS, s, LANES), BF16)
    head_spec = pl.BlockSpec((1, MLA_HEADS, ts, LANES), lambda bi, si: (bi, 0, si, 0))
    q_scale = (MLA_NOPE + MLA_ROPE) ** -0.5 * LOG2E
    dil_scale = HEAD_DIM ** -0.5 * LOG2E
    return pl.pallas_call(
        functools.partial(_even_in_kernel, q_scale=q_scale, dil_scale=dil_scale),
        out_shape=(head_out, head_out, head_out,
                   jax.ShapeDtypeStruct((b, s, wb.shape[1]), BF16)),
        grid=(b, n_s),
        in_specs=[
            pl.BlockSpec((1, ts, d), lambda bi, si: (bi, si, 0)),
            _const_spec(g.shape), _const_spec(wa.shape), _const_spec(wb.shape),
            _const_spec(gq.shape), _const_spec(gkv.shape),
            _const_spec(wuq.shape), _const_spec(wuk.shape), _const_spec(wuv.shape),
            pl.BlockSpec((ts, LANES), lambda bi, si: (si, 0)),
            pl.BlockSpec((ts, LANES), lambda bi, si: (si, 0)),
        ],
        out_specs=(head_spec, head_spec, head_spec,
                   pl.BlockSpec((1, ts, wb.shape[1]), lambda bi, si: (bi, si, 0))),
        compiler_params=_cparams(("parallel", "parallel")),
        name="even_in_proj",
    )(x, g, wa, wb, gq, gkv, wuq, wuk, wuv, cos_t, sin_t)


def _odd_in_kernel(x_ref, g_ref, wq_ref, wk_ref, wv_ref, gq_ref, gk_ref, cos_ref, sin_ref,
                   q_ref, k_ref, v_ref, *, q_scale):
    ts = x_ref.shape[1]
    xn = _rmsnorm(x_ref[0], g_ref[...]).astype(BF16)
    q_all = _dot(xn, wq_ref[...])
    k_all = _dot(xn, wk_ref[...])
    v_all = _dot(xn, wv_ref[...])
    cos_t = cos_ref[...]
    sin_t = sin_ref[...]
    first = _first_half_mask(ts)
    lane = lax.broadcasted_iota(jnp.int32, (ts, LANES), 1)
    ones_col = (lane == HEAD_DIM).astype(F32)

    def head_norm(t, gain):
        ms = jnp.sum(t * t, axis=-1, keepdims=True) * (1.0 / HEAD_DIM)
        return t * lax.rsqrt(ms + NORM_EPS) * gain

    gq = gq_ref[...] * q_scale
    gk = gk_ref[...]
    for h in range(GQA_HEADS):
        sl = slice(h * LANES, (h + 1) * LANES)
        q_ref[0, h] = _rope_slab(head_norm(q_all[:, sl], gq), cos_t, sin_t, first).astype(BF16)
    for h in range(GQA_KV_HEADS):
        sl = slice(h * LANES, (h + 1) * LANES)
        k_ref[0, h] = (_rope_slab(head_norm(k_all[:, sl], gk), cos_t, sin_t, first) + ones_col).astype(BF16)
        v_ref[0, h] = (v_all[:, sl] + ones_col).astype(BF16)


def _odd_in_proj(x, g, wq, wk, wv, gq, gk, cos_t, sin_t, *, ts):
    b, s, d = x.shape
    n_s = s // ts
    q_scale = HEAD_DIM ** -0.5 * LOG2E
    return pl.pallas_call(
        functools.partial(_odd_in_kernel, q_scale=q_scale),
        out_shape=(jax.ShapeDtypeStruct((b, GQA_HEADS, s, LANES), BF16),
                   jax.ShapeDtypeStruct((b, GQA_KV_HEADS, s, LANES), BF16),
                   jax.ShapeDtypeStruct((b, GQA_KV_HEADS, s, LANES), BF16)),
        grid=(b, n_s),
        in_specs=[
            pl.BlockSpec((1, ts, d), lambda bi, si: (bi, si, 0)),
            _const_spec(g.shape), _const_spec(wq.shape), _const_spec(wk.shape),
            _const_spec(wv.shape), _const_spec(gq.shape), _const_spec(gk.shape),
            pl.BlockSpec((ts, LANES), lambda bi, si: (si, 0)),
            pl.BlockSpec((ts, LANES), lambda bi, si: (si, 0)),
        ],
        out_specs=(pl.BlockSpec((1, GQA_HEADS, ts, LANES), lambda bi, si: (bi, 0, si, 0)),
                   pl.BlockSpec((1, GQA_KV_HEADS, ts, LANES), lambda bi, si: (bi, 0, si, 0)),
                   pl.BlockSpec((1, GQA_KV_HEADS, ts, LANES), lambda bi, si: (bi, 0, si, 0))),
        compiler_params=_cparams(("parallel", "parallel")),
        name="odd_in_proj",
    )(x, g, wq, wk, wv, gq, gk, cos_t, sin_t)


_NT_DIMS = (((1,), (1,)), ((), ()))


def _attn_kernel(q_ref, k_ref, v_ref, o_ref, kmax_scr, m_scr, acc_scr,
                 *, hb, rep, tk, tk_fast, bias_lane, fast_limit):
    tq = q_ref.shape[2]
    s_len = k_ref.shape[2]
    m_rows = rep * tq

    @pl.when(pl.program_id(2) == 0)
    def _():
        for h in range(hb):
            def kbody(j, best, h=h):
                start = pl.multiple_of(j * tk, tk)
                kc = k_ref[0, h, pl.ds(start, tk), :].astype(F32)
                kn2 = jnp.sum(kc * kc, axis=-1, keepdims=True)
                return jnp.maximum(best, jnp.max(kn2, axis=0, keepdims=True))

            best = lax.fori_loop(0, s_len // tk, kbody, jnp.zeros((1, 1), F32))
            kmax_scr[h] = jnp.broadcast_to(best, (8, LANES))

    lane = lax.broadcasted_iota(jnp.int32, (m_rows, LANES), 1)
    outs = []
    for h in range(hb):
        q2 = q_ref[0, h * rep:(h + 1) * rep].reshape(m_rows, LANES)
        qf = q2.astype(F32)
        qn2 = jnp.sum(qf * qf, axis=-1, keepdims=True)
        bound = jnp.sqrt(qn2 * kmax_scr[h][0:1, 0:1]) * BOUND_SLACK
        use_fast = jnp.max(bound) <= fast_limit
        acc_scr[...] = jnp.zeros((m_rows, LANES), F32)

        @pl.when(use_fast)
        def _(h=h, qf=qf, bound=bound):
            q_aug = jnp.where(lane == bias_lane, -bound, qf).astype(BF16)

            def body(j, carry):
                start = pl.multiple_of(j * tk_fast, tk_fast)
                kc = k_ref[0, h, pl.ds(start, tk_fast), :]
                vc = v_ref[0, h, pl.ds(start, tk_fast), :]
                s = lax.dot_general(q_aug, kc, _NT_DIMS, preferred_element_type=F32)
                acc_scr[...] += _dot(jnp.exp2(s).astype(BF16), vc)
                return carry

            lax.fori_loop(0, s_len // tk_fast, body, 0)

        @pl.when(jnp.logical_not(use_fast))
        def _(h=h, q2=q2):
            m_scr[...] = jnp.full((m_rows, 1), MASKED, F32)

            def body(j, carry):
                start = pl.multiple_of(j * tk, tk)
                kc = k_ref[0, h, pl.ds(start, tk), :]
                vc = v_ref[0, h, pl.ds(start, tk), :]
                s = lax.dot_general(q2, kc, _NT_DIMS, preferred_element_type=F32)
                m_old = m_scr[...]
                m_new = jnp.maximum(m_old, jnp.max(s, axis=-1, keepdims=True))
                p = jnp.exp2(s - m_new)
                alpha = jnp.exp2(m_old - m_new)
                acc_scr[...] = alpha * acc_scr[...] + _dot(p.astype(BF16), vc)
                m_scr[...] = m_new
                return carry

            lax.fori_loop(0, s_len // tk, body, 0)

        acc = acc_scr[...]
        o = acc[:, :HEAD_DIM] / acc[:, HEAD_DIM:HEAD_DIM + 1]
        for r in range(rep):
            outs.append(o[r * tq:(r + 1) * tq])
    o_ref[0] = jnp.concatenate(outs, axis=-1).astype(o_ref.dtype)


def _attention(q, k, v, *, hb, tq, tk, tk_fast, bias_lane, fast_limit=FAST_BOUND_LIMIT):
    b, hq, s, _ = q.shape
    hk = k.shape[1]
    rep = hq // hk
    width = hb * rep * HEAD_DIM
    m_rows = rep * tq
    return pl.pallas_call(
        functools.partial(_attn_kernel, hb=hb, rep=rep, tk=tk, tk_fast=tk_fast,
                          bias_lane=bias_lane, fast_limit=fast_limit),
        out_shape=jax.ShapeDtypeStruct((b, s, hq * HEAD_DIM), BF16),
        grid=(b, hk // hb, s // tq),
        in_specs=[
            pl.BlockSpec((1, hb * rep, tq, LANES), lambda bi, gi, qi: (bi, gi, qi, 0)),
            pl.BlockSpec((1, hb, s, LANES), lambda bi, gi, qi: (bi, gi, 0, 0)),
            pl.BlockSpec((1, hb, s, LANES), lambda bi, gi, qi: (bi, gi, 0, 0)),
        ],
        out_specs=pl.BlockSpec((1, tq, width), lambda bi, gi, qi: (bi, qi, gi)),
        scratch_shapes=[pltpu.VMEM((hb, 8, LANES), F32), pltpu.VMEM((m_rows, 1), F32),
                        pltpu.VMEM((m_rows, LANES), F32)],
        compiler_params=_cparams(("parallel", "parallel", "arbitrary")),
        name="dense_attention",
    )(q, k, v)


def _dilated_kernel(q_ref, k_ref, v_ref, o_ref, lse_ref, *, dilation, slopes, qb, win):
    length = k_ref.shape[1]
    u0 = pl.program_id(2) * qb
    start = jnp.clip(u0 - DIL_HALF, 0, length - win)
    start = pl.multiple_of(start, DIL_HALF)
    kw = k_ref[0, pl.ds(start, win), :]
    vw = v_ref[0, pl.ds(start, win), :]
    q = q_ref[0]
    row = lax.broadcasted_iota(jnp.int32, (qb, win), 0)
    col = lax.broadcasted_iota(jnp.int32, (qb, win), 1)
    rel = col - row + (start - u0)
    dist = jnp.abs(rel)
    valid = dist <= DIL_HALF
    distf = dist.astype(F32)
    lane = lax.broadcasted_iota(jnp.int32, (qb, DIL_WIDTH), 1)
    o_all = jnp.zeros((qb, DIL_WIDTH), F32)
    lse_all = jnp.zeros((qb, DIL_WIDTH), F32)
    for sl in range(DIL_SLOTS):
        in_slot = (lane >= sl * HEAD_DIM) & (lane < (sl + 1) * HEAD_DIM)
        q_s = jnp.where(in_slot, q, jnp.zeros_like(q))
        s = lax.dot_general(q_s, kw, (((1,), (1,)), ((), ())), preferred_element_type=F32)
        s = jnp.where(valid, s - (slopes[sl] * dilation * LOG2E) * distf, MASKED)
        m = jnp.max(s, axis=-1, keepdims=True)
        e = jnp.exp2(s - m)
        den = jnp.sum(e, axis=-1, keepdims=True)
        o = _dot((e / den).astype(BF16), vw)
        lse = (m + jnp.log2(den)) * LN2
        o_all = jnp.where(in_slot, o, o_all)
        lse_all = jnp.where(in_slot, lse, lse_all)
    o_ref[0] = o_all.astype(o_ref.dtype)
    lse_ref[0] = lse_all


def _dilated_group(zb, group, *, qb_max):
    b, s, cols = zb.shape
    _, dilation = DIL_PAIRS[group]
    length = s // dilation
    qb = min(qb_max, length)
    win = min(length, qb + 2 * DIL_HALF)
    nblk = cols // DIL_WIDTH
    view = zb.reshape(b, length, dilation * cols)
    slopes = tuple(2.0 ** (-8.0 * (group * DIL_SLOTS + i + 1) / DIL_HEADS) for i in range(DIL_SLOTS))
    o, lse = pl.pallas_call(
        functools.partial(_dilated_kernel, dilation=dilation, slopes=slopes, qb=qb, win=win),
        out_shape=(jax.ShapeDtypeStruct((b, length, dilation * DIL_WIDTH), BF16),
                   jax.ShapeDtypeStruct((b, length, dilation * DIL_WIDTH), F32)),
        grid=(b, dilation, length // qb),
        in_specs=[
            pl.BlockSpec((1, qb, DIL_WIDTH), lambda bi, r, j: (bi, j, r * nblk + group)),
            pl.BlockSpec((1, length, DIL_WIDTH), lambda bi, r, j: (bi, 0, r * nblk + 3 + group)),
            pl.BlockSpec((1, length, DIL_WIDTH), lambda bi, r, j: (bi, 0, r * nblk + 6 + group)),
        ],
        out_specs=(pl.BlockSpec((1, qb, DIL_WIDTH), lambda bi, r, j: (bi, j, r)),
                   pl.BlockSpec((1, qb, DIL_WIDTH), lambda bi, r, j: (bi, j, r))),
        compiler_params=_cparams(("parallel", "parallel", "arbitrary")),
        name=f"dilated_attention_g{group}",
    )(view, view, view)
    return o.reshape(b, s, DIL_WIDTH), lse.reshape(b, s, DIL_WIDTH)


def _ffn_tail(x1, gf_ref, wg_ref, wu_ref, wd_ref, gfin_ref, o_ref, *, chunks, final):
    hn = _rmsnorm(x1, gf_ref[...]).astype(BF16)
    hidden = wg_ref.shape[1]
    hc = hidden // chunks
    down = None
    for c in range(chunks):
        sl = slice(c * hc, (c + 1) * hc)
        gate = _dot(hn, wg_ref[:, sl])
        up = _dot(hn, wu_ref[:, sl])
        act = gate * (1.0 / (1.0 + jnp.exp(-gate))) * up
        part = _dot(act.astype(BF16), wd_ref[sl, :])
        down = part if down is None else down + part
    y = x1 + down
    if final:
        y = _rmsnorm(y, gfin_ref[...])
    o_ref[0] = y


def _even_out_kernel(x_ref, oa_ref, o0_ref, o1_ref, o2_ref, l0_ref, l1_ref, l2_ref,
                     woa_ref, wob_ref, gf_ref, wg_ref, wu_ref, wd_ref, gfin_ref, o_ref,
                     *, chunks, final):
    l0, l1, l2 = l0_ref[0], l1_ref[0], l2_ref[0]
    mx = jnp.maximum(jnp.maximum(l0, l1), l2)
    e0, e1, e2 = jnp.exp(l0 - mx), jnp.exp(l1 - mx), jnp.exp(l2 - mx)
    tot = e0 + e1 + e2
    comb = (e0 / tot) * o0_ref[0].astype(F32) + (e1 / tot) * o1_ref[0].astype(F32) \
        + (e2 / tot) * o2_ref[0].astype(F32)
    x1 = x_ref[0] + (_dot(oa_ref[0], woa_ref[...]) + _dot(comb.astype(BF16), wob_ref[...]))
    _ffn_tail(x1, gf_ref, wg_ref, wu_ref, wd_ref, gfin_ref, o_ref, chunks=chunks, final=final)


def _odd_out_kernel(x_ref, o_in_ref, wo_ref, gf_ref, wg_ref, wu_ref, wd_ref, gfin_ref, o_ref,
                    *, chunks, final):
    x1 = x_ref[0] + _dot(o_in_ref[0], wo_ref[...])
    _ffn_tail(x1, gf_ref, wg_ref, wu_ref, wd_ref, gfin_ref, o_ref, chunks=chunks, final=final)


def _out_ffn(kernel_fn, x, acts, consts, *, ts, chunks, final):
    b, s, d = x.shape

    def row_spec(a):
        return pl.BlockSpec((1, ts, a.shape[-1]), lambda bi, si: (bi, si, 0))

    return pl.pallas_call(
        functools.partial(kernel_fn, chunks=chunks, final=final),
        out_shape=jax.ShapeDtypeStruct((b, s, d), F32),
        grid=(b, s // ts),
        in_specs=[row_spec(x)] + [row_spec(a) for a in acts] + [_const_spec(c.shape) for c in consts],
        out_specs=row_spec(x),
        compiler_params=_cparams(("parallel", "parallel")),
        name=kernel_fn.__name__.strip("_"),
    )(x, *acts, *consts)


def _rope_angles(pos, dim):
    freqs = ROPE_THETA ** (-jnp.arange(0, dim, 2, dtype=F32) / dim)
    ang = pos.astype(F32)[:, None] * freqs[None, :]
    return jnp.cos(ang), jnp.sin(ang)


def _mla_tables(s):
    cos, sin = _rope_angles(jnp.arange(s), MLA_ROPE)
    one = jnp.ones((s, MLA_NOPE), F32)
    tail = LANES - MLA_NOPE - MLA_ROPE
    cos_t = jnp.concatenate([one, cos, cos, jnp.ones((s, tail), F32)], axis=-1)
    sin_t = jnp.concatenate([jnp.zeros((s, MLA_NOPE), F32), -sin, sin, jnp.zeros((s, tail), F32)], axis=-1)
    return cos_t, sin_t


def _axial_tables(s):
    rows = s // GRID_W
    row_idx = jnp.broadcast_to(jnp.arange(rows)[:, None], (rows, GRID_W)).reshape(-1)
    col_idx = jnp.broadcast_to(jnp.arange(GRID_W)[None, :], (rows, GRID_W)).reshape(-1)
    cos_r, sin_r = _rope_angles(row_idx, HEAD_DIM // 2)
    cos_c, sin_c = _rope_angles(col_idx, HEAD_DIM // 2)
    pad = LANES - HEAD_DIM
    cos_t = jnp.concatenate([cos_r, cos_r, cos_c, cos_c, jnp.ones((s, pad), F32)], axis=-1)
    sin_t = jnp.concatenate([-sin_r, sin_r, -sin_c, sin_c, jnp.zeros((s, pad), F32)], axis=-1)
    return cos_t, sin_t


def _head_slabs(w, fill):
    kdim, heads, d = w.shape
    return jnp.pad(w, ((0, 0), (0, 0), (0, fill - d))).reshape(kdim, heads * fill).astype(BF16)


def _row(v, width=None):
    v = v.reshape(1, -1).astype(F32)
    if width is not None:
        v = jnp.pad(v, ((0, 0), (0, width - v.shape[1])))
    return v


def kernel(x, mix_norm_ab, w_in_ab, mla_q_norm, mla_kv_norm, mla_w_uq, mla_w_ukv, w_out_ab,
           mix_norm_c, gqa_w_q, gqa_w_kv, gqa_q_norm, gqa_k_norm, gqa_w_o,
           ffn_norm, ffn_w_in, ffn_w_out, final_norm):
    b, s, d = x.shape
    depth = ffn_norm.shape[0]
    hidden = ffn_w_out.shape[1]
    ts = min(512, s)
    in_a = MLA_Q_RANK + MLA_KV_RANK
    mla_cos, mla_sin = _mla_tables(s)
    ax_cos, ax_sin = _axial_tables(s)
    gfin = _row(final_norm)

    for layer in range(depth):
        i = layer // 2
        final = layer == depth - 1
        gf = _row(ffn_norm[layer])
        wg = ffn_w_in[layer][:, :hidden].astype(BF16)
        wu = ffn_w_in[layer][:, hidden:].astype(BF16)
        wd = ffn_w_out[layer].astype(BF16)
        if layer % 2 == 0:
            w_in = w_in_ab[i]
            wa = jnp.concatenate([
                w_in[:, :in_a], jnp.zeros((d, MLA_NOPE), F32), w_in[:, in_a:in_a + MLA_ROPE],
                jnp.zeros((d, LANES - MLA_NOPE - MLA_ROPE), F32)], axis=-1).astype(BF16)
            wb = w_in[:, in_a + MLA_ROPE:].astype(BF16)
            wuq = _head_slabs(mla_w_uq[i], LANES)
            wuk = _head_slabs(mla_w_ukv[i][:, :, :MLA_NOPE], LANES)
            wuv = _head_slabs(mla_w_ukv[i][:, :, MLA_NOPE:], LANES)
            q, k, v, zb = _even_in_proj(
                x, _row(mix_norm_ab[i]), wa, wb, _row(mla_q_norm[i]), _row(mla_kv_norm[i]),
                wuq, wuk, wuv, mla_cos, mla_sin, ts=ts)
            o_a = _attention(q, k, v, hb=2, tq=min(1024, s), tk=min(512, s), tk_fast=min(2048, s),
                             bias_lane=MLA_BIAS_LANE)
            outs, lses = zip(*[_dilated_group(zb, g, qb_max=256) for g in range(DIL_GROUPS)])
            n_a = MLA_HEADS * HEAD_DIM
            x = _out_ffn(_even_out_kernel, x, (o_a,) + outs + lses,
                         (w_out_ab[i][:n_a].astype(BF16), w_out_ab[i][n_a:].astype(BF16),
                          gf, wg, wu, wd, gfin), ts=ts, chunks=2, final=final)
        else:
            wq = _head_slabs(gqa_w_q[i].reshape(d, GQA_HEADS, HEAD_DIM), LANES)
            wkv = gqa_w_kv[i].reshape(d, 2, GQA_KV_HEADS, HEAD_DIM)
            wk = _head_slabs(wkv[:, 0], LANES)
            wv = _head_slabs(wkv[:, 1], LANES)
            q, k, v = _odd_in_proj(
                x, _row(mix_norm_c[i]), wq, wk, wv, _row(gqa_q_norm[i], LANES),
                _row(gqa_k_norm[i], LANES), ax_cos, ax_sin, ts=ts)
            o = _attention(q, k, v, hb=1, tq=min(256, s), tk=min(512, s), tk_fast=min(2048, s),
                           bias_lane=GQA_BIAS_LANE)
            x = _out_ffn(_odd_out_kernel, x, (o,),
                         (gqa_w_o[i].astype(BF16), gf, wg, wu, wd, gfin),
                         ts=ts, chunks=2, final=final)
    return x
```

```python
import functools
import math

import jax
import jax.numpy as jnp
from jax import lax
from jax.experimental import pallas as pl
from jax.experimental.pallas import tpu as pltpu

F32 = jnp.float32
BF16 = jnp.bfloat16

NORM_EPS = 1e-6
ROPE_THETA = 10000.0
GRID_W = 64
LOG2E = math.log2(math.e)
LN2 = math.log(2.0)
MASKED = -1e30

LANES = 128
HEAD_DIM = 64

MLA_HEADS = 8
MLA_Q_RANK = 384
MLA_KV_RANK = 256
MLA_NOPE = 64
MLA_ROPE = 32

DIL_PAIRS = ((128, 1), (512, 4), (2048, 16))
DIL_HALF = 64
DIL_SLOTS = 4
DIL_GROUPS = 3
DIL_HEADS = DIL_SLOTS * DIL_GROUPS
DIL_WIDTH = DIL_SLOTS * HEAD_DIM

GQA_HEADS = 16
GQA_KV_HEADS = 4

VT_ROWS = 80

MLA_BIAS_LANE = MLA_NOPE + MLA_ROPE
GQA_BIAS_LANE = HEAD_DIM

FAST_BOUND_LIMIT = 60.0
BOUND_SLACK = 1.02

VMEM_LIMIT = 56 * 1024 * 1024


def _cparams(sem):
    return pltpu.CompilerParams(dimension_semantics=sem, vmem_limit_bytes=VMEM_LIMIT)


def _const_spec(shape):
    nd = len(shape)
    return pl.BlockSpec(shape, lambda *_: (0,) * nd, pipeline_mode=pl.Buffered(1))


def _rmsnorm(x, g):
    ms = jnp.mean(x * x, axis=-1, keepdims=True)
    return x * lax.rsqrt(ms + NORM_EPS) * g


def _dot(a, b):
    return jnp.dot(a, b, preferred_element_type=F32)


def _rope_slab(t, cos_t, sin_t, first_half):
    partner = jnp.where(first_half, pltpu.roll(t, LANES - 16, 1), pltpu.roll(t, 16, 1))
    return t * cos_t + partner * sin_t


def _first_half_mask(rows):
    lane = lax.broadcasted_iota(jnp.int32, (rows, LANES), 1)
    return (lane % 32) < 16


def _even_in_kernel(x_ref, g_ref, wa_ref, wb_ref, gq_ref, gkv_ref, wuq_ref, wuk_ref, wuv_ref,
                    cos_ref, sin_ref, q_ref, k_ref, v_ref, zb_ref, *, q_scale, dil_scale):
    ts = x_ref.shape[1]
    xn = _rmsnorm(x_ref[0], g_ref[...]).astype(BF16)
    za = _dot(xn, wa_ref[...])
    zb = _dot(xn, wb_ref[...])
    nq = DIL_GROUPS * DIL_WIDTH
    zb_ref[0, :, :nq] = (zb[:, :nq] * dil_scale).astype(BF16)
    zb_ref[0, :, nq:] = zb[:, nq:].astype(BF16)

    cq = _rmsnorm(za[:, :MLA_Q_RANK], gq_ref[...]).astype(BF16)
    ckv = _rmsnorm(za[:, MLA_Q_RANK:MLA_Q_RANK + MLA_KV_RANK], gkv_ref[...]).astype(BF16)
    q_all = _dot(cq, wuq_ref[...])
    k_all = _dot(ckv, wuk_ref[...])
    v_all = _dot(ckv, wuv_ref[...])

    cos_t = cos_ref[...]
    sin_t = sin_ref[...]
    first = _first_half_mask(ts)
    lane = lax.broadcasted_iota(jnp.int32, (ts, LANES), 1)
    ones_col = (lane == HEAD_DIM).astype(F32)
    k_rope = _rope_slab(za[:, MLA_Q_RANK + MLA_KV_RANK:], cos_t, sin_t, first) \
        + (lane == MLA_BIAS_LANE).astype(F32)
    for h in range(MLA_HEADS):
        sl = slice(h * LANES, (h + 1) * LANES)
        q_ref[0, h] = _rope_slab(q_all[:, sl] * q_scale, cos_t, sin_t, first).astype(BF16)
        k_ref[0, h] = (k_all[:, sl] + k_rope).astype(BF16)
        v_ref[0, h] = (v_all[:, sl] + ones_col).T[:VT_ROWS].astype(BF16)


def _even_in_proj(x, g, wa, wb, gq, gkv, wuq, wuk, wuv, cos_t, sin_t, *, ts):
    b, s, d = x.shape
    n_s = s // ts
    head_out = jax.ShapeDtypeStruct((b, MLA_HEADS, s, LANES), BF16)
    head_spec = pl.BlockSpec((1, MLA_HEADS, ts, LANES), lambda bi, si: (bi, 0, si, 0))
    q_scale = (MLA_NOPE + MLA_ROPE) ** -0.5 * LOG2E
    dil_scale = HEAD_DIM ** -0.5 * LOG2E
    return pl.pallas_call(
        functools.partial(_even_in_kernel, q_scale=q_scale, dil_scale=dil_scale),
        out_shape=(head_out, head_out, jax.ShapeDtypeStruct((b, MLA_HEADS, VT_ROWS, s), BF16),
                   jax.ShapeDtypeStruct((b, s, wb.shape[1]), BF16)),
        grid=(b, n_s),
        in_specs=[
            pl.BlockSpec((1, ts, d), lambda bi, si: (bi, si, 0)),
            _const_spec(g.shape), _const_spec(wa.shape), _const_spec(wb.shape),
            _const_spec(gq.shape), _const_spec(gkv.shape),
            _const_spec(wuq.shape), _const_spec(wuk.shape), _const_spec(wuv.shape),
            pl.BlockSpec((ts, LANES), lambda bi, si: (si, 0)),
            pl.BlockSpec((ts, LANES), lambda bi, si: (si, 0)),
        ],
        out_specs=(head_spec, head_spec,
                   pl.BlockSpec((1, MLA_HEADS, VT_ROWS, ts), lambda bi, si: (bi, 0, 0, si)),
                   pl.BlockSpec((1, ts, wb.shape[1]), lambda bi, si: (bi, si, 0))),
        compiler_params=_cparams(("parallel", "parallel")),
        name="even_in_proj",
    )(x, g, wa, wb, gq, gkv, wuq, wuk, wuv, cos_t, sin_t)


def _odd_in_kernel(x_ref, g_ref, wq_ref, wk_ref, wv_ref, gq_ref, gk_ref, cos_ref, sin_ref,
                   q_ref, k_ref, v_ref, *, q_scale):
    ts = x_ref.shape[1]
    xn = _rmsnorm(x_ref[0], g_ref[...]).astype(BF16)
    q_all = _dot(xn, wq_ref[...])
    k_all = _dot(xn, wk_ref[...])
    v_all = _dot(xn, wv_ref[...])
    cos_t = cos_ref[...]
    sin_t = sin_ref[...]
    first = _first_half_mask(ts)
    lane = lax.broadcasted_iota(jnp.int32, (ts, LANES), 1)
    ones_col = (lane == HEAD_DIM).astype(F32)

    def head_norm(t, gain):
        ms = jnp.sum(t * t, axis=-1, keepdims=True) * (1.0 / HEAD_DIM)
        return t * lax.rsqrt(ms + NORM_EPS) * gain

    gq = gq_ref[...] * q_scale
    gk = gk_ref[...]
    for h in range(GQA_HEADS):
        sl = slice(h * LANES, (h + 1) * LANES)
        q_ref[0, h] = _rope_slab(head_norm(q_all[:, sl], gq), cos_t, sin_t, first).astype(BF16)
    for h in range(GQA_KV_HEADS):
        sl = slice(h * LANES, (h + 1) * LANES)
        k_ref[0, h] = (_rope_slab(head_norm(k_all[:, sl], gk), cos_t, sin_t, first) + ones_col).astype(BF16)
        v_ref[0, h] = (v_all[:, sl] + ones_col).T[:VT_ROWS].astype(BF16)


def _odd_in_proj(x, g, wq, wk, wv, gq, gk, cos_t, sin_t, *, ts):
    b, s, d = x.shape
    n_s = s // ts
    q_scale = HEAD_DIM ** -0.5 * LOG2E
    return pl.pallas_call(
        functools.partial(_odd_in_kernel, q_scale=q_scale),
        out_shape=(jax.ShapeDtypeStruct((b, GQA_HEADS, s, LANES), BF16),
                   jax.ShapeDtypeStruct((b, GQA_KV_HEADS, s, LANES), BF16),
                   jax.ShapeDtypeStruct((b, GQA_KV_HEADS, VT_ROWS, s), BF16)),
        grid=(b, n_s),
        in_specs=[
            pl.BlockSpec((1, ts, d), lambda bi, si: (bi, si, 0)),
            _const_spec(g.shape), _const_spec(wq.shape), _const_spec(wk.shape),
            _const_spec(wv.shape), _const_spec(gq.shape), _const_spec(gk.shape),
            pl.BlockSpec((ts, LANES), lambda bi, si: (si, 0)),
            pl.BlockSpec((ts, LANES), lambda bi, si: (si, 0)),
        ],
        out_specs=(pl.BlockSpec((1, GQA_HEADS, ts, LANES), lambda bi, si: (bi, 0, si, 0)),
                   pl.BlockSpec((1, GQA_KV_HEADS, ts, LANES), lambda bi, si: (bi, 0, si, 0)),
                   pl.BlockSpec((1, GQA_KV_HEADS, VT_ROWS, ts), lambda bi, si: (bi, 0, 0, si))),
        compiler_params=_cparams(("parallel", "parallel")),
        name="odd_in_proj",
    )(x, g, wq, wk, wv, gq, gk, cos_t, sin_t)


_NT_DIMS = (((1,), (1,)), ((), ()))


def _attn_kernel(q_ref, k_ref, v_ref, o_ref, kmax_scr, m_scr, acc_scr,
                 *, hb, rep, tk, tk_fast, bias_lane, fast_limit):
    tq = q_ref.shape[2]
    s_len = k_ref.shape[2]
    m_rows = rep * tq

    @pl.when(pl.program_id(2) == 0)
    def _():
        for h in range(hb):
            def kbody(j, best, h=h):
                start = pl.multiple_of(j * tk, tk)
                kc = k_ref[0, h, pl.ds(start, tk), :].astype(F32)
                kn2 = jnp.sum(kc * kc, axis=-1, keepdims=True)
                return jnp.maximum(best, jnp.max(kn2, axis=0, keepdims=True))

            best = lax.fori_loop(0, s_len // tk, kbody, jnp.zeros((1, 1), F32))
            kmax_scr[h] = jnp.broadcast_to(best, (8, LANES))

    lane = lax.broadcasted_iota(jnp.int32, (m_rows, LANES), 1)
    outs = []
    for h in range(hb):
        q2 = q_ref[0, h * rep:(h + 1) * rep].reshape(m_rows, LANES)
        qf = q2.astype(F32)
        qn2 = jnp.sum(qf * qf, axis=-1, keepdims=True)
        bound = jnp.sqrt(qn2 * kmax_scr[h][0:1, 0:1]) * BOUND_SLACK
        use_fast = jnp.max(bound) <= fast_limit
        acc_scr[...] = jnp.zeros((VT_ROWS, m_rows), F32)

        @pl.when(use_fast)
        def _(h=h, qf=qf, bound=bound):
            q_aug = jnp.where(lane == bias_lane, -bound, qf).astype(BF16)

            def body(j, carry):
                start = pl.multiple_of(j * tk_fast, tk_fast)
                kc = k_ref[0, h, pl.ds(start, tk_fast), :]
                vtc = v_ref[0, h, :, pl.ds(start, tk_fast)]
                st = lax.dot_general(kc, q_aug, _NT_DIMS, preferred_element_type=F32)
                acc_scr[...] += _dot(vtc, jnp.exp2(st).astype(BF16))
                return carry

            lax.fori_loop(0, s_len // tk_fast, body, 0)

        @pl.when(jnp.logical_not(use_fast))
        def _(h=h, q2=q2):
            m_scr[...] = jnp.full((1, m_rows), MASKED, F32)

            def body(j, carry):
                start = pl.multiple_of(j * tk, tk)
                kc = k_ref[0, h, pl.ds(start, tk), :]
                vtc = v_ref[0, h, :, pl.ds(start, tk)]
                st = lax.dot_general(kc, q2, _NT_DIMS, preferred_element_type=F32)
                m_old = m_scr[...]
                m_new = jnp.maximum(m_old, jnp.max(st, axis=0, keepdims=True))
                p = jnp.exp2(st - m_new)
                alpha = jnp.exp2(m_old - m_new)
                acc_scr[...] = alpha * acc_scr[...] + _dot(vtc, p.astype(BF16))
                m_scr[...] = m_new
                return carry

            lax.fori_loop(0, s_len // tk, body, 0)

        acc = acc_scr[...]
        o = (acc[:HEAD_DIM] / acc[HEAD_DIM:HEAD_DIM + 1]).T
        for r in range(rep):
            outs.append(o[r * tq:(r + 1) * tq])
    o_ref[0] = jnp.concatenate(outs, axis=-1).astype(o_ref.dtype)


def _attention(q, k, v, *, hb, tq, tk, tk_fast, bias_lane, fast_limit=FAST_BOUND_LIMIT):
    b, hq, s, _ = q.shape
    hk = k.shape[1]
    rep = hq // hk
    width = hb * rep * HEAD_DIM
    m_rows = rep * tq
    return pl.pallas_call(
        functools.partial(_attn_kernel, hb=hb, rep=rep, tk=tk, tk_fast=tk_fast,
                          bias_lane=bias_lane, fast_limit=fast_limit),
        out_shape=jax.ShapeDtypeStruct((b, s, hq * HEAD_DIM), BF16),
        grid=(b, hk // hb, s // tq),
        in_specs=[
            pl.BlockSpec((1, hb * rep, tq, LANES), lambda bi, gi, qi: (bi, gi, qi, 0)),
            pl.BlockSpec((1, hb, s, LANES), lambda bi, gi, qi: (bi, gi, 0, 0)),
            pl.BlockSpec((1, hb, VT_ROWS, s), lambda bi, gi, qi: (bi, gi, 0, 0)),
        ],
        out_specs=pl.BlockSpec((1, tq, width), lambda bi, gi, qi: (bi, qi, gi)),
        scratch_shapes=[pltpu.VMEM((hb, 8, LANES), F32), pltpu.VMEM((1, m_rows), F32),
                        pltpu.VMEM((VT_ROWS, m_rows), F32)],
        compiler_params=_cparams(("parallel", "parallel", "arbitrary")),
        name="dense_attention",
    )(q, k, v)


def _dilated_kernel(q_ref, k_ref, v_ref, o_ref, lse_ref, *, dilation, slopes, qb, win):
    length = k_ref.shape[1]
    u0 = pl.program_id(2) * qb
    start = jnp.clip(u0 - DIL_HALF, 0, length - win)
    start = pl.multiple_of(start, DIL_HALF)
    kw = k_ref[0, pl.ds(start, win), :]
    vw = v_ref[0, pl.ds(start, win), :]
    q = q_ref[0]
    row = lax.broadcasted_iota(jnp.int32, (qb, win), 0)
    col = lax.broadcasted_iota(jnp.int32, (qb, win), 1)
    rel = col - row + (start - u0)
    dist = jnp.abs(rel)
    valid = dist <= DIL_HALF
    distf = dist.astype(F32)
    lane = lax.broadcasted_iota(jnp.int32, (qb, DIL_WIDTH), 1)
    o_all = jnp.zeros((qb, DIL_WIDTH), F32)
    lse_all = jnp.zeros((qb, DIL_WIDTH), F32)
    for sl in range(DIL_SLOTS):
        in_slot = (lane >= sl * HEAD_DIM) & (lane < (sl + 1) * HEAD_DIM)
        q_s = jnp.where(in_slot, q, jnp.zeros_like(q))
        s = lax.dot_general(q_s, kw, (((1,), (1,)), ((), ())), preferred_element_type=F32)
        s = jnp.where(valid, s - (slopes[sl] * dilation * LOG2E) * distf, MASKED)
        m = jnp.max(s, axis=-1, keepdims=True)
        e = jnp.exp2(s - m)
        den = jnp.sum(e, axis=-1, keepdims=True)
        o = _dot((e / den).astype(BF16), vw)
        lse = (m + jnp.log2(den)) * LN2
        o_all = jnp.where(in_slot, o, o_all)
        lse_all = jnp.where(in_slot, lse, lse_all)
    o_ref[0] = o_all.astype(o_ref.dtype)
    lse_ref[0] = lse_all


def _dilated_group(zb, group, *, qb_max):
    b, s, cols = zb.shape
    _, dilation = DIL_PAIRS[group]
    length = s // dilation
    qb = min(qb_max, length)
    win = min(length, qb + 2 * DIL_HALF)
    nblk = cols // DIL_WIDTH
    view = zb.reshape(b, length, dilation * cols)
    slopes = tuple(2.0 ** (-8.0 * (group * DIL_SLOTS + i + 1) / DIL_HEADS) for i in range(DIL_SLOTS))
    o, lse = pl.pallas_call(
        functools.partial(_dilated_kernel, dilation=dilation, slopes=slopes, qb=qb, win=win),
        out_shape=(jax.ShapeDtypeStruct((b, length, dilation * DIL_WIDTH), BF16),
                   jax.ShapeDtypeStruct((b, length, dilation * DIL_WIDTH), F32)),
        grid=(b, dilation, length // qb),
        in_specs=[
            pl.BlockSpec((1, qb, DIL_WIDTH), lambda bi, r, j: (bi, j, r * nblk + group)),
            pl.BlockSpec((1, length, DIL_WIDTH), lambda bi, r, j: (bi, 0, r * nblk + 3 + group)),
            pl.BlockSpec((1, length, DIL_WIDTH), lambda bi, r, j: (bi, 0, r * nblk + 6 + group)),
        ],
        out_specs=(pl.BlockSpec((1, qb, DIL_WIDTH), lambda bi, r, j: (bi, j, r)),
                   pl.BlockSpec((1, qb, DIL_WIDTH), lambda bi, r, j: (bi, j, r))),
        compiler_params=_cparams(("parallel", "parallel", "arbitrary")),
        name=f"dilated_attention_g{group}",
    )(view, view, view)
    return o.reshape(b, s, DIL_WIDTH), lse.reshape(b, s, DIL_WIDTH)


def _ffn_tail(x1, gf_ref, wg_ref, wu_ref, wd_ref, gfin_ref, o_ref, *, chunks, final):
    hn = _rmsnorm(x1, gf_ref[...]).astype(BF16)
    hidden = wg_ref.shape[1]
    hc = hidden // chunks
    down = None
    for c in range(chunks):
        sl = slice(c * hc, (c + 1) * hc)
        gate = _dot(hn, wg_ref[:, sl])
        up = _dot(hn, wu_ref[:, sl])
        act = gate * (1.0 / (1.0 + jnp.exp(-gate))) * up
        part = _dot(act.astype(BF16), wd_ref[sl, :])
        down = part if down is None else down + part
    y = x1 + down
    if final:
        y = _rmsnorm(y, gfin_ref[...])
    o_ref[0] = y


def _even_out_kernel(x_ref, oa_ref, o0_ref, o1_ref, o2_ref, l0_ref, l1_ref, l2_ref,
                     woa_ref, wob_ref, gf_ref, wg_ref, wu_ref, wd_ref, gfin_ref, o_ref,
                     *, chunks, final):
    l0, l1, l2 = l0_ref[0], l1_ref[0], l2_ref[0]
    mx = jnp.maximum(jnp.maximum(l0, l1), l2)
    e0, e1, e2 = jnp.exp(l0 - mx), jnp.exp(l1 - mx), jnp.exp(l2 - mx)
    tot = e0 + e1 + e2
    comb = (e0 / tot) * o0_ref[0].astype(F32) + (e1 / tot) * o1_ref[0].astype(F32) \
        + (e2 / tot) * o2_ref[0].astype(F32)
    x1 = x_ref[0] + (_dot(oa_ref[0], woa_ref[...]) + _dot(comb.astype(BF16), wob_ref[...]))
    _ffn_tail(x1, gf_ref, wg_ref, wu_ref, wd_ref, gfin_ref, o_ref, chunks=chunks, final=final)


def _odd_out_kernel(x_ref, o_in_ref, wo_ref, gf_ref, wg_ref, wu_ref, wd_ref, gfin_ref, o_ref,
                    *, chunks, final):
    x1 = x_ref[0] + _dot(o_in_ref[0], wo_ref[...])
    _ffn_tail(x1, gf_ref, wg_ref, wu_ref, wd_ref, gfin_ref, o_ref, chunks=chunks, final=final)


def _out_ffn(kernel_fn, x, acts, consts, *, ts, chunks, final):
    b, s, d = x.shape

    def row_spec(a):
        return pl.BlockSpec((1, ts, a.shape[-1]), lambda bi, si: (bi, si, 0))

    return pl.pallas_call(
        functools.partial(kernel_fn, chunks=chunks, final=final),
        out_shape=jax.ShapeDtypeStruct((b, s, d), F32),
        grid=(b, s // ts),
        in_specs=[row_spec(x)] + [row_spec(a) for a in acts] + [_const_spec(c.shape) for c in consts],
        out_specs=row_spec(x),
        compiler_params=_cparams(("parallel", "parallel")),
        name=kernel_fn.__name__.strip("_"),
    )(x, *acts, *consts)


def _rope_angles(pos, dim):
    freqs = ROPE_THETA ** (-jnp.arange(0, dim, 2, dtype=F32) / dim)
    ang = pos.astype(F32)[:, None] * freqs[None, :]
    return jnp.cos(ang), jnp.sin(ang)


def _mla_tables(s):
    cos, sin = _rope_angles(jnp.arange(s), MLA_ROPE)
    one = jnp.ones((s, MLA_NOPE), F32)
    tail = LANES - MLA_NOPE - MLA_ROPE
    cos_t = jnp.concatenate([one, cos, cos, jnp.ones((s, tail), F32)], axis=-1)
    sin_t = jnp.concatenate([jnp.zeros((s, MLA_NOPE), F32), -sin, sin, jnp.zeros((s, tail), F32)], axis=-1)
    return cos_t, sin_t


def _axial_tables(s):
    rows = s // GRID_W
    row_idx = jnp.broadcast_to(jnp.arange(rows)[:, None], (rows, GRID_W)).reshape(-1)
    col_idx = jnp.broadcast_to(jnp.arange(GRID_W)[None, :], (rows, GRID_W)).reshape(-1)
    cos_r, sin_r = _rope_angles(row_idx, HEAD_DIM // 2)
    cos_c, sin_c = _rope_angles(col_idx, HEAD_DIM // 2)
    pad = LANES - HEAD_DIM
    cos_t = jnp.concatenate([cos_r, cos_r, cos_c, cos_c, jnp.ones((s, pad), F32)], axis=-1)
    sin_t = jnp.concatenate([-sin_r, sin_r, -sin_c, sin_c, jnp.zeros((s, pad), F32)], axis=-1)
    return cos_t, sin_t


def _head_slabs(w, fill):
    kdim, heads, d = w.shape
    return jnp.pad(w, ((0, 0), (0, 0), (0, fill - d))).reshape(kdim, heads * fill).astype(BF16)


def _row(v, width=None):
    v = v.reshape(1, -1).astype(F32)
    if width is not None:
        v = jnp.pad(v, ((0, 0), (0, width - v.shape[1])))
    return v


def kernel(x, mix_norm_ab, w_in_ab, mla_q_norm, mla_kv_norm, mla_w_uq, mla_w_ukv, w_out_ab,
           mix_norm_c, gqa_w_q, gqa_w_kv, gqa_q_norm, gqa_k_norm, gqa_w_o,
           ffn_norm, ffn_w_in, ffn_w_out, final_norm):
    b, s, d = x.shape
    depth = ffn_norm.shape[0]
    hidden = ffn_w_out.shape[1]
    ts = min(512, s)
    in_a = MLA_Q_RANK + MLA_KV_RANK
    mla_cos, mla_sin = _mla_tables(s)
    ax_cos, ax_sin = _axial_tables(s)
    gfin = _row(final_norm)

    for layer in range(depth):
        i = layer // 2
        final = layer == depth - 1
        gf = _row(ffn_norm[layer])
        wg = ffn_w_in[layer][:, :hidden].astype(BF16)
        wu = ffn_w_in[layer][:, hidden:].astype(BF16)
        wd = ffn_w_out[layer].astype(BF16)
        if layer % 2 == 0:
            w_in = w_in_ab[i]
            wa = jnp.concatenate([
                w_in[:, :in_a], jnp.zeros((d, MLA_NOPE), F32), w_in[:, in_a:in_a + MLA_ROPE],
                jnp.zeros((d, LANES - MLA_NOPE - MLA_ROPE), F32)], axis=-1).astype(BF16)
            wb = w_in[:, in_a + MLA_ROPE:].astype(BF16)
            wuq = _head_slabs(mla_w_uq[i], LANES)
            wuk = _head_slabs(mla_w_ukv[i][:, :, :MLA_NOPE], LANES)
            wuv = _head_slabs(mla_w_ukv[i][:, :, MLA_NOPE:], LANES)
            q, k, v, zb = _even_in_proj(
                x, _row(mix_norm_ab[i]), wa, wb, _row(mla_q_norm[i]), _row(mla_kv_norm[i]),
                wuq, wuk, wuv, mla_cos, mla_sin, ts=ts)
            o_a = _attention(q, k, v, hb=2, tq=min(1024, s), tk=min(512, s), tk_fast=min(2048, s),
                             bias_lane=MLA_BIAS_LANE)
            outs, lses = zip(*[_dilated_group(zb, g, qb_max=256) for g in range(DIL_GROUPS)])
            n_a = MLA_HEADS * HEAD_DIM
            x = _out_ffn(_even_out_kernel, x, (o_a,) + outs + lses,
                         (w_out_ab[i][:n_a].astype(BF16), w_out_ab[i][n_a:].astype(BF16),
                          gf, wg, wu, wd, gfin), ts=ts, chunks=2, final=final)
        else:
            wq = _head_slabs(gqa_w_q[i].reshape(d, GQA_HEADS, HEAD_DIM), LANES)
            wkv = gqa_w_kv[i].reshape(d, 2, GQA_KV_HEADS, HEAD_DIM)
            wk = _head_slabs(wkv[:, 0], LANES)
            wv = _head_slabs(wkv[:, 1], LANES)
            q, k, v = _odd_in_proj(
                x, _row(mix_norm_c[i]), wq, wk, wv, _row(gqa_q_norm[i], LANES),
                _row(gqa_k_norm[i], LANES), ax_cos, ax_sin, ts=ts)
            o = _attention(q, k, v, hb=1, tq=min(256, s), tk=min(512, s), tk_fast=min(2048, s),
                           bias_lane=GQA_BIAS_LANE)
            x = _out_ffn(_odd_out_kernel, x, (o,),
                         (gqa_w_o[i].astype(BF16), gf, wg, wu, wd, gfin),
                         ts=ts, chunks=2, final=final)
    return x
```

```python
import functools
import math

import jax
import jax.numpy as jnp
from jax import lax
from jax.experimental import pallas as pl
from jax.experimental.pallas import tpu as pltpu

F32 = jnp.float32
BF16 = jnp.bfloat16

NORM_EPS = 1e-6
ROPE_THETA = 10000.0
GRID_W = 64
LOG2E = math.log2(math.e)
LN2 = math.log(2.0)
MASKED = -1e30

LANES = 128
HEAD_DIM = 64

MLA_HEADS = 8
MLA_Q_RANK = 384
MLA_KV_RANK = 256
MLA_NOPE = 64
MLA_ROPE = 32

DIL_PAIRS = ((128, 1), (512, 4), (2048, 16))
DIL_HALF = 64
DIL_SLOTS = 4
DIL_GROUPS = 3
DIL_HEADS = DIL_SLOTS * DIL_GROUPS
DIL_WIDTH = DIL_SLOTS * HEAD_DIM

GQA_HEADS = 16
GQA_KV_HEADS = 4

VT_ROWS = 80

MLA_BIAS_LANE = MLA_NOPE + MLA_ROPE
GQA_BIAS_LANE = HEAD_DIM

FAST_BOUND_LIMIT = 60.0
BOUND_SLACK = 1.02

VMEM_LIMIT = 56 * 1024 * 1024


def _cparams(sem):
    return pltpu.CompilerParams(dimension_semantics=sem, vmem_limit_bytes=VMEM_LIMIT)


def _const_spec(shape):
    nd = len(shape)
    return pl.BlockSpec(shape, lambda *_: (0,) * nd, pipeline_mode=pl.Buffered(1))


def _rmsnorm(x, g):
    ms = jnp.mean(x * x, axis=-1, keepdims=True)
    return x * lax.rsqrt(ms + NORM_EPS) * g


def _dot(a, b):
    return jnp.dot(a, b, preferred_element_type=F32)


def _rope_slab(t, cos_t, sin_t, first_half):
    partner = jnp.where(first_half, pltpu.roll(t, LANES - 16, 1), pltpu.roll(t, 16, 1))
    return t * cos_t + partner * sin_t


def _first_half_mask(rows):
    lane = lax.broadcasted_iota(jnp.int32, (rows, LANES), 1)
    return (lane % 32) < 16


def _even_in_kernel(x_ref, g_ref, wa_ref, wb_ref, gq_ref, gkv_ref, wuq_ref, wuk_ref, wuv_ref,
                    cos_ref, sin_ref, q_ref, k_ref, v_ref, z0_ref, z1_ref, z2_ref,
                    *, q_scale, dil_scale):
    ts = x_ref.shape[1]
    xn = _rmsnorm(x_ref[0], g_ref[...]).astype(BF16)
    za = _dot(xn, wa_ref[...])
    zb = _dot(xn, wb_ref[...])
    nq = DIL_GROUPS * DIL_WIDTH
    for g, zg_ref in enumerate((z0_ref, z1_ref, z2_ref)):
        lo = g * DIL_WIDTH
        zg_ref[0, :, :DIL_WIDTH] = (zb[:, lo:lo + DIL_WIDTH] * dil_scale).astype(BF16)
        zg_ref[0, :, DIL_WIDTH:2 * DIL_WIDTH] = zb[:, nq + lo:nq + lo + DIL_WIDTH].astype(BF16)
        zg_ref[0, :, 2 * DIL_WIDTH:] = zb[:, 2 * nq + lo:2 * nq + lo + DIL_WIDTH].astype(BF16)

    cq = _rmsnorm(za[:, :MLA_Q_RANK], gq_ref[...]).astype(BF16)
    ckv = _rmsnorm(za[:, MLA_Q_RANK:MLA_Q_RANK + MLA_KV_RANK], gkv_ref[...]).astype(BF16)
    q_all = _dot(cq, wuq_ref[...])
    k_all = _dot(ckv, wuk_ref[...])
    v_all = _dot(ckv, wuv_ref[...])

    cos_t = cos_ref[...]
    sin_t = sin_ref[...]
    first = _first_half_mask(ts)
    lane = lax.broadcasted_iota(jnp.int32, (ts, LANES), 1)
    ones_col = (lane == HEAD_DIM).astype(F32)
    k_rope = _rope_slab(za[:, MLA_Q_RANK + MLA_KV_RANK:], cos_t, sin_t, first) \
        + (lane == MLA_BIAS_LANE).astype(F32)
    for h in range(MLA_HEADS):
        sl = slice(h * LANES, (h + 1) * LANES)
        q_ref[0, h] = _rope_slab(q_all[:, sl] * q_scale, cos_t, sin_t, first).astype(BF16)
        k_ref[0, h] = (k_all[:, sl] + k_rope).astype(BF16)
        v_ref[0, h] = (v_all[:, sl] + ones_col).T[:VT_ROWS].astype(BF16)


def _even_in_proj(x, g, wa, wb, gq, gkv, wuq, wuk, wuv, cos_t, sin_t, *, ts):
    b, s, d = x.shape
    n_s = s // ts
    head_out = jax.ShapeDtypeStruct((b, MLA_HEADS, s, LANES), BF16)
    head_spec = pl.BlockSpec((1, MLA_HEADS, ts, LANES), lambda bi, si: (bi, 0, si, 0))
    q_scale = (MLA_NOPE + MLA_ROPE) ** -0.5 * LOG2E
    dil_scale = HEAD_DIM ** -0.5 * LOG2E
    zg_out = jax.ShapeDtypeStruct((b, s, 3 * DIL_WIDTH), BF16)
    zg_spec = pl.BlockSpec((1, ts, 3 * DIL_WIDTH), lambda bi, si: (bi, si, 0))
    return pl.pallas_call(
        functools.partial(_even_in_kernel, q_scale=q_scale, dil_scale=dil_scale),
        out_shape=(head_out, head_out, jax.ShapeDtypeStruct((b, MLA_HEADS, VT_ROWS, s), BF16),
                   zg_out, zg_out, zg_out),
        grid=(b, n_s),
        in_specs=[
            pl.BlockSpec((1, ts, d), lambda bi, si: (bi, si, 0)),
            _const_spec(g.shape), _const_spec(wa.shape), _const_spec(wb.shape),
            _const_spec(gq.shape), _const_spec(gkv.shape),
            _const_spec(wuq.shape), _const_spec(wuk.shape), _const_spec(wuv.shape),
            pl.BlockSpec((ts, LANES), lambda bi, si: (si, 0)),
            pl.BlockSpec((ts, LANES), lambda bi, si: (si, 0)),
        ],
        out_specs=(head_spec, head_spec,
                   pl.BlockSpec((1, MLA_HEADS, VT_ROWS, ts), lambda bi, si: (bi, 0, 0, si)),
                   zg_spec, zg_spec, zg_spec),
        compiler_params=_cparams(("parallel", "parallel")),
        name="even_in_proj",
    )(x, g, wa, wb, gq, gkv, wuq, wuk, wuv, cos_t, sin_t)


def _odd_in_kernel(x_ref, g_ref, wq_ref, wk_ref, wv_ref, gq_ref, gk_ref, cos_ref, sin_ref,
                   q_ref, k_ref, v_ref, *, q_scale):
    ts = x_ref.shape[1]
    xn = _rmsnorm(x_ref[0], g_ref[...]).astype(BF16)
    q_all = _dot(xn, wq_ref[...])
    k_all = _dot(xn, wk_ref[...])
    v_all = _dot(xn, wv_ref[...])
    cos_t = cos_ref[...]
    sin_t = sin_ref[...]
    first = _first_half_mask(ts)
    lane = lax.broadcasted_iota(jnp.int32, (ts, LANES), 1)
    ones_col = (lane == HEAD_DIM).astype(F32)

    def head_norm(t, gain):
        ms = jnp.sum(t * t, axis=-1, keepdims=True) * (1.0 / HEAD_DIM)
        return t * lax.rsqrt(ms + NORM_EPS) * gain

    gq = gq_ref[...] * q_scale
    gk = gk_ref[...]
    for h in range(GQA_HEADS):
        sl = slice(h * LANES, (h + 1) * LANES)
        q_ref[0, h] = _rope_slab(head_norm(q_all[:, sl], gq), cos_t, sin_t, first).astype(BF16)
    for h in range(GQA_KV_HEADS):
        sl = slice(h * LANES, (h + 1) * LANES)
        k_ref[0, h] = (_rope_slab(head_norm(k_all[:, sl], gk), cos_t, sin_t, first) + ones_col).astype(BF16)
        v_ref[0, h] = (v_all[:, sl] + ones_col).T[:VT_ROWS].astype(BF16)


def _odd_in_proj(x, g, wq, wk, wv, gq, gk, cos_t, sin_t, *, ts):
    b, s, d = x.shape
    n_s = s // ts
    q_scale = HEAD_DIM ** -0.5 * LOG2E
    return pl.pallas_call(
        functools.partial(_odd_in_kernel, q_scale=q_scale),
        out_shape=(jax.ShapeDtypeStruct((b, GQA_HEADS, s, LANES), BF16),
                   jax.ShapeDtypeStruct((b, GQA_KV_HEADS, s, LANES), BF16),
                   jax.ShapeDtypeStruct((b, GQA_KV_HEADS, VT_ROWS, s), BF16)),
        grid=(b, n_s),
        in_specs=[
            pl.BlockSpec((1, ts, d), lambda bi, si: (bi, si, 0)),
            _const_spec(g.shape), _const_spec(wq.shape), _const_spec(wk.shape),
            _const_spec(wv.shape), _const_spec(gq.shape), _const_spec(gk.shape),
            pl.BlockSpec((ts, LANES), lambda bi, si: (si, 0)),
            pl.BlockSpec((ts, LANES), lambda bi, si: (si, 0)),
        ],
        out_specs=(pl.BlockSpec((1, GQA_HEADS, ts, LANES), lambda bi, si: (bi, 0, si, 0)),
                   pl.BlockSpec((1, GQA_KV_HEADS, ts, LANES), lambda bi, si: (bi, 0, si, 0)),
                   pl.BlockSpec((1, GQA_KV_HEADS, VT_ROWS, ts), lambda bi, si: (bi, 0, 0, si))),
        compiler_params=_cparams(("parallel", "parallel")),
        name="odd_in_proj",
    )(x, g, wq, wk, wv, gq, gk, cos_t, sin_t)


_NT_DIMS = (((1,), (1,)), ((), ()))


def _attn_kernel(q_ref, k_ref, v_ref, o_ref, kmax_scr, m_scr, acc_scr, p_scr,
                 *, hb, rep, tk, tk_fast, bias_lane, fast_limit):
    tq = q_ref.shape[2]
    s_len = k_ref.shape[2]
    m_rows = rep * tq

    @pl.when(pl.program_id(2) == 0)
    def _():
        for h in range(hb):
            def kbody(j, best, h=h):
                start = pl.multiple_of(j * tk, tk)
                kc = k_ref[0, h, pl.ds(start, tk), :].astype(F32)
                kn2 = jnp.sum(kc * kc, axis=-1, keepdims=True)
                return jnp.maximum(best, jnp.max(kn2, axis=0, keepdims=True))

            best = lax.fori_loop(0, s_len // tk, kbody, jnp.zeros((1, 1), F32))
            kmax_scr[h] = jnp.broadcast_to(best, (8, LANES))

    lane = lax.broadcasted_iota(jnp.int32, (m_rows, LANES), 1)
    outs = []
    for h in range(hb):
        q2 = q_ref[0, h * rep:(h + 1) * rep].reshape(m_rows, LANES)
        qf = q2.astype(F32)
        qn2 = jnp.sum(qf * qf, axis=-1, keepdims=True)
        bound = jnp.sqrt(qn2 * kmax_scr[h][0:1, 0:1]) * BOUND_SLACK
        use_fast = jnp.max(bound) <= fast_limit
        acc_scr[...] = jnp.zeros((VT_ROWS, m_rows), F32)

        @pl.when(use_fast)
        def _(h=h, qf=qf, bound=bound):
            q_aug = jnp.where(lane == bias_lane, -bound, qf).astype(BF16)

            def probs(j):
                start = pl.multiple_of(j * tk_fast, tk_fast)
                kc = k_ref[0, h, pl.ds(start, tk_fast), :]
                st = lax.dot_general(kc, q_aug, _NT_DIMS, preferred_element_type=F32)
                return jnp.exp2(st).astype(BF16)

            def accumulate(j, slot):
                start = pl.multiple_of(j * tk_fast, tk_fast)
                acc_scr[...] += _dot(v_ref[0, h, :, pl.ds(start, tk_fast)], p_scr[slot])

            n_chunks = s_len // tk_fast
            p_scr[0] = probs(0)

            def body(i, carry):
                nxt = probs(2 * i + 1)
                accumulate(2 * i, 0)
                p_scr[1] = nxt
                nxt = probs(2 * i + 2)
                accumulate(2 * i + 1, 1)
                p_scr[0] = nxt
                return carry

            lax.fori_loop(0, n_chunks // 2 - 1, body, 0)
            nxt = probs(n_chunks - 1)
            accumulate(n_chunks - 2, 0)
            p_scr[1] = nxt
            accumulate(n_chunks - 1, 1)

        @pl.when(jnp.logical_not(use_fast))
        def _(h=h, q2=q2):
            m_scr[...] = jnp.full((1, m_rows), MASKED, F32)

            def body(j, carry):
                start = pl.multiple_of(j * tk, tk)
                kc = k_ref[0, h, pl.ds(start, tk), :]
                vtc = v_ref[0, h, :, pl.ds(start, tk)]
                st = lax.dot_general(kc, q2, _NT_DIMS, preferred_element_type=F32)
                m_old = m_scr[...]
                m_new = jnp.maximum(m_old, jnp.max(st, axis=0, keepdims=True))
                p = jnp.exp2(st - m_new)
                alpha = jnp.exp2(m_old - m_new)
                acc_scr[...] = alpha * acc_scr[...] + _dot(vtc, p.astype(BF16))
                m_scr[...] = m_new
                return carry

            lax.fori_loop(0, s_len // tk, body, 0)

        acc = acc_scr[...]
        o = (acc[:HEAD_DIM] / acc[HEAD_DIM:HEAD_DIM + 1]).T
        for r in range(rep):
            outs.append(o[r * tq:(r + 1) * tq])
    o_ref[0] = jnp.concatenate(outs, axis=-1).astype(o_ref.dtype)


def _attention(q, k, v, *, hb, tq, tk, tk_fast, bias_lane, fast_limit=FAST_BOUND_LIMIT):
    b, hq, s, _ = q.shape
    hk = k.shape[1]
    rep = hq // hk
    width = hb * rep * HEAD_DIM
    m_rows = rep * tq
    assert (s // tk_fast) % 2 == 0, "the two-slot pipeline consumes key chunks in pairs"
    return pl.pallas_call(
        functools.partial(_attn_kernel, hb=hb, rep=rep, tk=tk, tk_fast=tk_fast,
                          bias_lane=bias_lane, fast_limit=fast_limit),
        out_shape=jax.ShapeDtypeStruct((b, s, hq * HEAD_DIM), BF16),
        grid=(b, hk // hb, s // tq),
        in_specs=[
            pl.BlockSpec((1, hb * rep, tq, LANES), lambda bi, gi, qi: (bi, gi, qi, 0)),
            pl.BlockSpec((1, hb, s, LANES), lambda bi, gi, qi: (bi, gi, 0, 0)),
            pl.BlockSpec((1, hb, VT_ROWS, s), lambda bi, gi, qi: (bi, gi, 0, 0)),
        ],
        out_specs=pl.BlockSpec((1, tq, width), lambda bi, gi, qi: (bi, qi, gi)),
        scratch_shapes=[pltpu.VMEM((hb, 8, LANES), F32), pltpu.VMEM((1, m_rows), F32),
                        pltpu.VMEM((VT_ROWS, m_rows), F32), pltpu.VMEM((2, tk_fast, m_rows), BF16)],
        compiler_params=_cparams(("parallel", "parallel", "arbitrary")),
        name="dense_attention",
    )(q, k, v)


def _dilated_kernel(q_ref, k_ref, v_ref, o_ref, lse_ref, *, dilation, slopes, qb, win):
    length = k_ref.shape[1]
    u0 = pl.program_id(2) * qb
    start = jnp.clip(u0 - DIL_HALF, 0, length - win)
    start = pl.multiple_of(start, DIL_HALF)
    kw = k_ref[0, pl.ds(start, win), :]
    vw = v_ref[0, pl.ds(start, win), :]
    q = q_ref[0]
    row = lax.broadcasted_iota(jnp.int32, (qb, win), 0)
    col = lax.broadcasted_iota(jnp.int32, (qb, win), 1)
    rel = col - row + (start - u0)
    dist = jnp.abs(rel)
    valid = dist <= DIL_HALF
    distf = dist.astype(F32)
    lane = lax.broadcasted_iota(jnp.int32, (qb, DIL_WIDTH), 1)
    o_all = jnp.zeros((qb, DIL_WIDTH), F32)
    lse_all = jnp.zeros((qb, DIL_WIDTH), F32)
    for sl in range(DIL_SLOTS):
        in_slot = (lane >= sl * HEAD_DIM) & (lane < (sl + 1) * HEAD_DIM)
        q_s = jnp.where(in_slot, q, jnp.zeros_like(q))
        s = lax.dot_general(q_s, kw, (((1,), (1,)), ((), ())), preferred_element_type=F32)
        s = jnp.where(valid, s - (slopes[sl] * dilation * LOG2E) * distf, MASKED)
        m = jnp.max(s, axis=-1, keepdims=True)
        e = jnp.exp2(s - m)
        den = jnp.sum(e, axis=-1, keepdims=True)
        o = _dot((e / den).astype(BF16), vw)
        lse = (m + jnp.log2(den)) * LN2
        o_all = jnp.where(in_slot, o, o_all)
        lse_all = jnp.where(in_slot, lse, lse_all)
    o_ref[0] = o_all.astype(o_ref.dtype)
    lse_ref[0] = lse_all


def _dilated_group(zg, group, *, qb_max):
    b, s, cols = zg.shape
    _, dilation = DIL_PAIRS[group]
    length = s // dilation
    qb = min(qb_max, length)
    win = min(length, qb + 2 * DIL_HALF)
    nblk = cols // DIL_WIDTH
    view = zg.reshape(b, length, dilation * cols)
    slopes = tuple(2.0 ** (-8.0 * (group * DIL_SLOTS + i + 1) / DIL_HEADS) for i in range(DIL_SLOTS))
    o, lse = pl.pallas_call(
        functools.partial(_dilated_kernel, dilation=dilation, slopes=slopes, qb=qb, win=win),
        out_shape=(jax.ShapeDtypeStruct((b, length, dilation * DIL_WIDTH), BF16),
                   jax.ShapeDtypeStruct((b, length, dilation * DIL_WIDTH), F32)),
        grid=(b, dilation, length // qb),
        in_specs=[
            pl.BlockSpec((1, qb, DIL_WIDTH), lambda bi, r, j: (bi, j, r * nblk)),
            pl.BlockSpec((1, length, DIL_WIDTH), lambda bi, r, j: (bi, 0, r * nblk + 1)),
            pl.BlockSpec((1, length, DIL_WIDTH), lambda bi, r, j: (bi, 0, r * nblk + 2)),
        ],
        out_specs=(pl.BlockSpec((1, qb, DIL_WIDTH), lambda bi, r, j: (bi, j, r)),
                   pl.BlockSpec((1, qb, DIL_WIDTH), lambda bi, r, j: (bi, j, r))),
        compiler_params=_cparams(("parallel", "parallel", "arbitrary")),
        name=f"dilated_attention_g{group}",
    )(view, view, view)
    return o.reshape(b, s, DIL_WIDTH), lse.reshape(b, s, DIL_WIDTH)


def _ffn_tail(x1, gf_ref, wg_ref, wu_ref, wd_ref, gfin_ref, o_ref, *, chunks, final):
    hn = _rmsnorm(x1, gf_ref[...]).astype(BF16)
    hidden = wg_ref.shape[1]
    hc = hidden // chunks
    down = None
    for c in range(chunks):
        sl = slice(c * hc, (c + 1) * hc)
        gate = _dot(hn, wg_ref[:, sl])
        up = _dot(hn, wu_ref[:, sl])
        act = gate * (1.0 / (1.0 + jnp.exp(-gate))) * up
        part = _dot(act.astype(BF16), wd_ref[sl, :])
        down = part if down is None else down + part
    y = x1 + down
    if final:
        y = _rmsnorm(y, gfin_ref[...])
    o_ref[0] = y


def _even_out_kernel(x_ref, oa_ref, o0_ref, o1_ref, o2_ref, l0_ref, l1_ref, l2_ref,
                     woa_ref, wob_ref, gf_ref, wg_ref, wu_ref, wd_ref, gfin_ref, o_ref,
                     *, chunks, final):
    l0, l1, l2 = l0_ref[0], l1_ref[0], l2_ref[0]
    mx = jnp.maximum(jnp.maximum(l0, l1), l2)
    e0, e1, e2 = jnp.exp(l0 - mx), jnp.exp(l1 - mx), jnp.exp(l2 - mx)
    tot = e0 + e1 + e2
    comb = (e0 / tot) * o0_ref[0].astype(F32) + (e1 / tot) * o1_ref[0].astype(F32) \
        + (e2 / tot) * o2_ref[0].astype(F32)
    x1 = x_ref[0] + (_dot(oa_ref[0], woa_ref[...]) + _dot(comb.astype(BF16), wob_ref[...]))
    _ffn_tail(x1, gf_ref, wg_ref, wu_ref, wd_ref, gfin_ref, o_ref, chunks=chunks, final=final)


def _odd_out_kernel(x_ref, o_in_ref, wo_ref, gf_ref, wg_ref, wu_ref, wd_ref, gfin_ref, o_ref,
                    *, chunks, final):
    x1 = x_ref[0] + _dot(o_in_ref[0], wo_ref[...])
    _ffn_tail(x1, gf_ref, wg_ref, wu_ref, wd_ref, gfin_ref, o_ref, chunks=chunks, final=final)


def _out_ffn(kernel_fn, x, acts, consts, *, ts, chunks, final):
    b, s, d = x.shape

    def row_spec(a):
        return pl.BlockSpec((1, ts, a.shape[-1]), lambda bi, si: (bi, si, 0))

    return pl.pallas_call(
        functools.partial(kernel_fn, chunks=chunks, final=final),
        out_shape=jax.ShapeDtypeStruct((b, s, d), F32),
        grid=(b, s // ts),
        in_specs=[row_spec(x)] + [row_spec(a) for a in acts] + [_const_spec(c.shape) for c in consts],
        out_specs=row_spec(x),
        compiler_params=_cparams(("parallel", "parallel")),
        name=kernel_fn.__name__.strip("_"),
    )(x, *acts, *consts)


def _rope_angles(pos, dim):
    freqs = ROPE_THETA ** (-jnp.arange(0, dim, 2, dtype=F32) / dim)
    ang = pos.astype(F32)[:, None] * freqs[None, :]
    return jnp.cos(ang), jnp.sin(ang)


def _mla_tables(s):
    cos, sin = _rope_angles(jnp.arange(s), MLA_ROPE)
    one = jnp.ones((s, MLA_NOPE), F32)
    tail = LANES - MLA_NOPE - MLA_ROPE
    cos_t = jnp.concatenate([one, cos, cos, jnp.ones((s, tail), F32)], axis=-1)
    sin_t = jnp.concatenate([jnp.zeros((s, MLA_NOPE), F32), -sin, sin, jnp.zeros((s, tail), F32)], axis=-1)
    return cos_t, sin_t


def _axial_tables(s):
    rows = s // GRID_W
    row_idx = jnp.broadcast_to(jnp.arange(rows)[:, None], (rows, GRID_W)).reshape(-1)
    col_idx = jnp.broadcast_to(jnp.arange(GRID_W)[None, :], (rows, GRID_W)).reshape(-1)
    cos_r, sin_r = _rope_angles(row_idx, HEAD_DIM // 2)
    cos_c, sin_c = _rope_angles(col_idx, HEAD_DIM // 2)
    pad = LANES - HEAD_DIM
    cos_t = jnp.concatenate([cos_r, cos_r, cos_c, cos_c, jnp.ones((s, pad), F32)], axis=-1)
    sin_t = jnp.concatenate([-sin_r, sin_r, -sin_c, sin_c, jnp.zeros((s, pad), F32)], axis=-1)
    return cos_t, sin_t


def _head_slabs(w, fill):
    kdim, heads, d = w.shape
    return jnp.pad(w, ((0, 0), (0, 0), (0, fill - d))).reshape(kdim, heads * fill).astype(BF16)


def _row(v, width=None):
    v = v.reshape(1, -1).astype(F32)
    if width is not None:
        v = jnp.pad(v, ((0, 0), (0, width - v.shape[1])))
    return v


def kernel(x, mix_norm_ab, w_in_ab, mla_q_norm, mla_kv_norm, mla_w_uq, mla_w_ukv, w_out_ab,
           mix_norm_c, gqa_w_q, gqa_w_kv, gqa_q_norm, gqa_k_norm, gqa_w_o,
           ffn_norm, ffn_w_in, ffn_w_out, final_norm):
    b, s, d = x.shape
    depth = ffn_norm.shape[0]
    hidden = ffn_w_out.shape[1]
    ts = min(512, s)
    in_a = MLA_Q_RANK + MLA_KV_RANK
    mla_cos, mla_sin = _mla_tables(s)
    ax_cos, ax_sin = _axial_tables(s)
    gfin = _row(final_norm)

    for layer in range(depth):
        i = layer // 2
        final = layer == depth - 1
        gf = _row(ffn_norm[layer])
        wg = ffn_w_in[layer][:, :hidden].astype(BF16)
        wu = ffn_w_in[layer][:, hidden:].astype(BF16)
        wd = ffn_w_out[layer].astype(BF16)
        if layer % 2 == 0:
            w_in = w_in_ab[i]
            wa = jnp.concatenate([
                w_in[:, :in_a], jnp.zeros((d, MLA_NOPE), F32), w_in[:, in_a:in_a + MLA_ROPE],
                jnp.zeros((d, LANES - MLA_NOPE - MLA_ROPE), F32)], axis=-1).astype(BF16)
            wb = w_in[:, in_a + MLA_ROPE:].astype(BF16)
            wuq = _head_slabs(mla_w_uq[i], LANES)
            wuk = _head_slabs(mla_w_ukv[i][:, :, :MLA_NOPE], LANES)
            wuv = _head_slabs(mla_w_ukv[i][:, :, MLA_NOPE:], LANES)
            q, k, v, *zgs = _even_in_proj(
                x, _row(mix_norm_ab[i]), wa, wb, _row(mla_q_norm[i]), _row(mla_kv_norm[i]),
                wuq, wuk, wuv, mla_cos, mla_sin, ts=ts)
            o_a = _attention(q, k, v, hb=2, tq=min(1024, s), tk=min(512, s), tk_fast=min(1024, s // 2),
                             bias_lane=MLA_BIAS_LANE)
            outs, lses = zip(*[_dilated_group(zgs[g], g, qb_max=256) for g in range(DIL_GROUPS)])
            n_a = MLA_HEADS * HEAD_DIM
            x = _out_ffn(_even_out_kernel, x, (o_a,) + outs + lses,
                         (w_out_ab[i][:n_a].astype(BF16), w_out_ab[i][n_a:].astype(BF16),
                          gf, wg, wu, wd, gfin), ts=ts, chunks=2, final=final)
        else:
            wq = _head_slabs(gqa_w_q[i].reshape(d, GQA_HEADS, HEAD_DIM), LANES)
            wkv = gqa_w_kv[i].reshape(d, 2, GQA_KV_HEADS, HEAD_DIM)
            wk = _head_slabs(wkv[:, 0], LANES)
            wv = _head_slabs(wkv[:, 1], LANES)
            q, k, v = _odd_in_proj(
                x, _row(mix_norm_c[i]), wq, wk, wv, _row(gqa_q_norm[i], LANES),
                _row(gqa_k_norm[i], LANES), ax_cos, ax_sin, ts=ts)
            o = _attention(q, k, v, hb=1, tq=min(256, s), tk=min(512, s), tk_fast=min(1024, s // 2),
                           bias_lane=GQA_BIAS_LANE)
            x = _out_ffn(_odd_out_kernel, x, (o,),
                         (gqa_w_o[i].astype(BF16), gf, wg, wu, wd, gfin),
                         ts=ts, chunks=2, final=final)
    return x
```

```python
import functools
import math

import jax
import jax.numpy as jnp
from jax import lax
from jax.experimental import pallas as pl
from jax.experimental.pallas import tpu as pltpu

F32 = jnp.float32
BF16 = jnp.bfloat16

NORM_EPS = 1e-6
ROPE_THETA = 10000.0
GRID_W = 64
LOG2E = math.log2(math.e)
LN2 = math.log(2.0)
MASKED = -1e30

LANES = 128
HEAD_DIM = 64

MLA_HEADS = 8
MLA_Q_RANK = 384
MLA_KV_RANK = 256
MLA_NOPE = 64
MLA_ROPE = 32

DIL_PAIRS = ((128, 1), (512, 4), (2048, 16))
DIL_HALF = 64
DIL_SLOTS = 4
DIL_GROUPS = 3
DIL_HEADS = DIL_SLOTS * DIL_GROUPS
DIL_WIDTH = DIL_SLOTS * HEAD_DIM

GQA_HEADS = 16
GQA_KV_HEADS = 4

VT_ROWS = 80

MLA_BIAS_LANE = MLA_NOPE + MLA_ROPE
GQA_BIAS_LANE = HEAD_DIM

FAST_BOUND_LIMIT = 60.0
BOUND_SLACK = 1.02

VMEM_LIMIT = 56 * 1024 * 1024


def _cparams(sem):
    return pltpu.CompilerParams(dimension_semantics=sem, vmem_limit_bytes=VMEM_LIMIT)


def _const_spec(shape):
    nd = len(shape)
    return pl.BlockSpec(shape, lambda *_: (0,) * nd, pipeline_mode=pl.Buffered(1))


def _rmsnorm(x, g):
    ms = jnp.mean(x * x, axis=-1, keepdims=True)
    return x * lax.rsqrt(ms + NORM_EPS) * g


def _dot(a, b):
    return jnp.dot(a, b, preferred_element_type=F32)


def _rope_slab(t, cos_t, sin_t, first_half):
    partner = jnp.where(first_half, pltpu.roll(t, LANES - 16, 1), pltpu.roll(t, 16, 1))
    return t * cos_t + partner * sin_t


def _first_half_mask(rows):
    lane = lax.broadcasted_iota(jnp.int32, (rows, LANES), 1)
    return (lane % 32) < 16


def _even_in_kernel(x_ref, g_ref, wa_ref, wb_ref, gq_ref, gkv_ref, wuq_ref, wuk_ref, wuv_ref,
                    cos_ref, sin_ref, q_ref, k_ref, v_ref, z0_ref, z1_ref, z2_ref,
                    *, q_scale, dil_scale):
    ts = x_ref.shape[1]
    xn = _rmsnorm(x_ref[0], g_ref[...]).astype(BF16)
    za = _dot(xn, wa_ref[...])
    zb = _dot(xn, wb_ref[...])
    nq = DIL_GROUPS * DIL_WIDTH
    for g, zg_ref in enumerate((z0_ref, z1_ref, z2_ref)):
        lo = g * DIL_WIDTH
        zg_ref[0, :, :DIL_WIDTH] = (zb[:, lo:lo + DIL_WIDTH] * dil_scale).astype(BF16)
        zg_ref[0, :, DIL_WIDTH:2 * DIL_WIDTH] = zb[:, nq + lo:nq + lo + DIL_WIDTH].astype(BF16)
        zg_ref[0, :, 2 * DIL_WIDTH:] = zb[:, 2 * nq + lo:2 * nq + lo + DIL_WIDTH].astype(BF16)

    cq = _rmsnorm(za[:, :MLA_Q_RANK], gq_ref[...]).astype(BF16)
    ckv = _rmsnorm(za[:, MLA_Q_RANK:MLA_Q_RANK + MLA_KV_RANK], gkv_ref[...]).astype(BF16)
    q_all = _dot(cq, wuq_ref[...])
    k_all = _dot(ckv, wuk_ref[...])
    v_all = _dot(ckv, wuv_ref[...])

    cos_t = cos_ref[...]
    sin_t = sin_ref[...]
    first = _first_half_mask(ts)
    lane = lax.broadcasted_iota(jnp.int32, (ts, LANES), 1)
    ones_col = (lane == HEAD_DIM).astype(F32)
    k_rope = _rope_slab(za[:, MLA_Q_RANK + MLA_KV_RANK:], cos_t, sin_t, first) \
        + (lane == MLA_BIAS_LANE).astype(F32)
    for h in range(MLA_HEADS):
        sl = slice(h * LANES, (h + 1) * LANES)
        q_ref[0, h] = _rope_slab(q_all[:, sl] * q_scale, cos_t, sin_t, first).astype(BF16)
        k_ref[0, h] = (k_all[:, sl] + k_rope).astype(BF16)
        v_ref[0, h] = (v_all[:, sl] + ones_col).T[:VT_ROWS].astype(BF16)


def _even_in_proj(x, g, wa, wb, gq, gkv, wuq, wuk, wuv, cos_t, sin_t, *, ts):
    b, s, d = x.shape
    n_s = s // ts
    head_out = jax.ShapeDtypeStruct((b, MLA_HEADS, s, LANES), BF16)
    head_spec = pl.BlockSpec((1, MLA_HEADS, ts, LANES), lambda bi, si: (bi, 0, si, 0))
    q_scale = (MLA_NOPE + MLA_ROPE) ** -0.5 * LOG2E
    dil_scale = HEAD_DIM ** -0.5 * LOG2E
    zg_out = jax.ShapeDtypeStruct((b, s, 3 * DIL_WIDTH), BF16)
    zg_spec = pl.BlockSpec((1, ts, 3 * DIL_WIDTH), lambda bi, si: (bi, si, 0))
    return pl.pallas_call(
        functools.partial(_even_in_kernel, q_scale=q_scale, dil_scale=dil_scale),
        out_shape=(head_out, head_out, jax.ShapeDtypeStruct((b, MLA_HEADS, VT_ROWS, s), BF16),
                   zg_out, zg_out, zg_out),
        grid=(b, n_s),
        in_specs=[
            pl.BlockSpec((1, ts, d), lambda bi, si: (bi, si, 0)),
            _const_spec(g.shape), _const_spec(wa.shape), _const_spec(wb.shape),
            _const_spec(gq.shape), _const_spec(gkv.shape),
            _const_spec(wuq.shape), _const_spec(wuk.shape), _const_spec(wuv.shape),
            pl.BlockSpec((ts, LANES), lambda bi, si: (si, 0)),
            pl.BlockSpec((ts, LANES), lambda bi, si: (si, 0)),
        ],
        out_specs=(head_spec, head_spec,
                   pl.BlockSpec((1, MLA_HEADS, VT_ROWS, ts), lambda bi, si: (bi, 0, 0, si)),
                   zg_spec, zg_spec, zg_spec),
        compiler_params=_cparams(("parallel", "parallel")),
        name="even_in_proj",
    )(x, g, wa, wb, gq, gkv, wuq, wuk, wuv, cos_t, sin_t)


def _odd_in_kernel(x_ref, g_ref, wq_ref, wk_ref, wv_ref, gq_ref, gk_ref, cos_ref, sin_ref,
                   q_ref, k_ref, v_ref, *, q_scale):
    ts = x_ref.shape[1]
    xn = _rmsnorm(x_ref[0], g_ref[...]).astype(BF16)
    q_all = _dot(xn, wq_ref[...])
    k_all = _dot(xn, wk_ref[...])
    v_all = _dot(xn, wv_ref[...])
    cos_t = cos_ref[...]
    sin_t = sin_ref[...]
    first = _first_half_mask(ts)
    lane = lax.broadcasted_iota(jnp.int32, (ts, LANES), 1)
    ones_col = (lane == HEAD_DIM).astype(F32)

    def head_norm(t, gain):
        ms = jnp.sum(t * t, axis=-1, keepdims=True) * (1.0 / HEAD_DIM)
        return t * lax.rsqrt(ms + NORM_EPS) * gain

    gq = gq_ref[...] * q_scale
    gk = gk_ref[...]
    for h in range(GQA_HEADS):
        sl = slice(h * LANES, (h + 1) * LANES)
        q_ref[0, h] = _rope_slab(head_norm(q_all[:, sl], gq), cos_t, sin_t, first).astype(BF16)
    for h in range(GQA_KV_HEADS):
        sl = slice(h * LANES, (h + 1) * LANES)
        k_ref[0, h] = (_rope_slab(head_norm(k_all[:, sl], gk), cos_t, sin_t, first) + ones_col).astype(BF16)
        v_ref[0, h] = (v_all[:, sl] + ones_col).T[:VT_ROWS].astype(BF16)


def _odd_in_proj(x, g, wq, wk, wv, gq, gk, cos_t, sin_t, *, ts):
    b, s, d = x.shape
    n_s = s // ts
    q_scale = HEAD_DIM ** -0.5 * LOG2E
    return pl.pallas_call(
        functools.partial(_odd_in_kernel, q_scale=q_scale),
        out_shape=(jax.ShapeDtypeStruct((b, GQA_HEADS, s, LANES), BF16),
                   jax.ShapeDtypeStruct((b, GQA_KV_HEADS, s, LANES), BF16),
                   jax.ShapeDtypeStruct((b, GQA_KV_HEADS, VT_ROWS, s), BF16)),
        grid=(b, n_s),
        in_specs=[
            pl.BlockSpec((1, ts, d), lambda bi, si: (bi, si, 0)),
            _const_spec(g.shape), _const_spec(wq.shape), _const_spec(wk.shape),
            _const_spec(wv.shape), _const_spec(gq.shape), _const_spec(gk.shape),
            pl.BlockSpec((ts, LANES), lambda bi, si: (si, 0)),
            pl.BlockSpec((ts, LANES), lambda bi, si: (si, 0)),
        ],
        out_specs=(pl.BlockSpec((1, GQA_HEADS, ts, LANES), lambda bi, si: (bi, 0, si, 0)),
                   pl.BlockSpec((1, GQA_KV_HEADS, ts, LANES), lambda bi, si: (bi, 0, si, 0)),
                   pl.BlockSpec((1, GQA_KV_HEADS, VT_ROWS, ts), lambda bi, si: (bi, 0, 0, si))),
        compiler_params=_cparams(("parallel", "parallel")),
        name="odd_in_proj",
    )(x, g, wq, wk, wv, gq, gk, cos_t, sin_t)


_NT_DIMS = (((1,), (1,)), ((), ()))


def _attn_kernel(q_ref, k_ref, v_ref, o_ref, kmax_scr, m_scr, acc_scr, p_scr,
                 *, hb, rep, tk, tk_fast, bias_lane, fast_limit):
    tq = q_ref.shape[2]
    s_len = k_ref.shape[2]
    m_rows = rep * tq

    @pl.when(pl.program_id(2) == 0)
    def _():
        for h in range(hb):
            def kbody(j, best, h=h):
                start = pl.multiple_of(j * tk, tk)
                kc = k_ref[0, h, pl.ds(start, tk), :].astype(F32)
                kn2 = jnp.sum(kc * kc, axis=-1, keepdims=True)
                return jnp.maximum(best, jnp.max(kn2, axis=0, keepdims=True))

            best = lax.fori_loop(0, s_len // tk, kbody, jnp.zeros((1, 1), F32))
            kmax_scr[h] = jnp.broadcast_to(best, (8, LANES))

    lane = lax.broadcasted_iota(jnp.int32, (m_rows, LANES), 1)
    outs = []
    for h in range(hb):
        q2 = q_ref[0, h * rep:(h + 1) * rep].reshape(m_rows, LANES)
        qf = q2.astype(F32)
        qn2 = _dot((qf * qf).astype(BF16), jnp.ones((LANES, LANES), BF16))
        bound = qn2 * lax.rsqrt(qn2 + 1e-30) * (jnp.sqrt(kmax_scr[h][0:1, :]) * BOUND_SLACK)
        use_fast = jnp.max(bound) <= fast_limit
        acc_scr[...] = jnp.zeros((VT_ROWS, m_rows), F32)

        @pl.when(use_fast)
        def _(h=h, qf=qf, bound=bound):
            q_aug = jnp.where(lane == bias_lane, -bound, qf).astype(BF16)

            def probs(j):
                start = pl.multiple_of(j * tk_fast, tk_fast)
                kc = k_ref[0, h, pl.ds(start, tk_fast), :]
                st = lax.dot_general(kc, q_aug, _NT_DIMS, preferred_element_type=F32)
                return jnp.exp2(st).astype(BF16)

            def accumulate(j, slot):
                start = pl.multiple_of(j * tk_fast, tk_fast)
                acc_scr[...] += _dot(v_ref[0, h, :, pl.ds(start, tk_fast)], p_scr[slot])

            n_chunks = s_len // tk_fast
            p_scr[0] = probs(0)

            def body(i, carry):
                nxt = probs(2 * i + 1)
                accumulate(2 * i, 0)
                p_scr[1] = nxt
                nxt = probs(2 * i + 2)
                accumulate(2 * i + 1, 1)
                p_scr[0] = nxt
                return carry

            lax.fori_loop(0, n_chunks // 2 - 1, body, 0)
            nxt = probs(n_chunks - 1)
            accumulate(n_chunks - 2, 0)
            p_scr[1] = nxt
            accumulate(n_chunks - 1, 1)

        @pl.when(jnp.logical_not(use_fast))
        def _(h=h, q2=q2):
            m_scr[...] = jnp.full((1, m_rows), MASKED, F32)

            def body(j, carry):
                start = pl.multiple_of(j * tk, tk)
                kc = k_ref[0, h, pl.ds(start, tk), :]
                vtc = v_ref[0, h, :, pl.ds(start, tk)]
                st = lax.dot_general(kc, q2, _NT_DIMS, preferred_element_type=F32)
                m_old = m_scr[...]
                m_new = jnp.maximum(m_old, jnp.max(st, axis=0, keepdims=True))
                p = jnp.exp2(st - m_new)
                alpha = jnp.exp2(m_old - m_new)
                acc_scr[...] = alpha * acc_scr[...] + _dot(vtc, p.astype(BF16))
                m_scr[...] = m_new
                return carry

            lax.fori_loop(0, s_len // tk, body, 0)

        acc = acc_scr[...]
        o = (acc[:HEAD_DIM] / acc[HEAD_DIM:HEAD_DIM + 1]).T
        for r in range(rep):
            outs.append(o[r * tq:(r + 1) * tq])
    o_ref[0] = jnp.concatenate(outs, axis=-1).astype(o_ref.dtype)


def _attention(q, k, v, *, hb, tq, tk, tk_fast, bias_lane, fast_limit=FAST_BOUND_LIMIT):
    b, hq, s, _ = q.shape
    hk = k.shape[1]
    rep = hq // hk
    width = hb * rep * HEAD_DIM
    m_rows = rep * tq
    assert (s // tk_fast) % 2 == 0, "the two-slot pipeline consumes key chunks in pairs"
    return pl.pallas_call(
        functools.partial(_attn_kernel, hb=hb, rep=rep, tk=tk, tk_fast=tk_fast,
                          bias_lane=bias_lane, fast_limit=fast_limit),
        out_shape=jax.ShapeDtypeStruct((b, s, hq * HEAD_DIM), BF16),
        grid=(b, hk // hb, s // tq),
        in_specs=[
            pl.BlockSpec((1, hb * rep, tq, LANES), lambda bi, gi, qi: (bi, gi, qi, 0)),
            pl.BlockSpec((1, hb, s, LANES), lambda bi, gi, qi: (bi, gi, 0, 0)),
            pl.BlockSpec((1, hb, VT_ROWS, s), lambda bi, gi, qi: (bi, gi, 0, 0)),
        ],
        out_specs=pl.BlockSpec((1, tq, width), lambda bi, gi, qi: (bi, qi, gi)),
        scratch_shapes=[pltpu.VMEM((hb, 8, LANES), F32), pltpu.VMEM((1, m_rows), F32),
                        pltpu.VMEM((VT_ROWS, m_rows), F32), pltpu.VMEM((2, tk_fast, m_rows), BF16)],
        compiler_params=_cparams(("parallel", "parallel", "arbitrary")),
        name="dense_attention",
    )(q, k, v)


def _dilated_kernel(q_ref, k_ref, v_ref, o_ref, lse_ref, *, dilation, slopes, qb, win):
    length = k_ref.shape[1]
    u0 = pl.program_id(2) * qb
    start = jnp.clip(u0 - DIL_HALF, 0, length - win)
    start = pl.multiple_of(start, DIL_HALF)
    kw = k_ref[0, pl.ds(start, win), :]
    vw = v_ref[0, pl.ds(start, win), :]
    q = q_ref[0]
    row = lax.broadcasted_iota(jnp.int32, (qb, win), 0)
    col = lax.broadcasted_iota(jnp.int32, (qb, win), 1)
    rel = col - row + (start - u0)
    dist = jnp.abs(rel)
    dist_m = jnp.where(dist <= DIL_HALF, dist.astype(F32), -MASKED)
    lane = lax.broadcasted_iota(jnp.int32, (qb, DIL_WIDTH), 1)
    o_all = jnp.zeros((qb, DIL_WIDTH), F32)
    lse_all = jnp.zeros((qb, DIL_WIDTH), F32)
    for sl in range(DIL_SLOTS):
        in_slot = (lane >= sl * HEAD_DIM) & (lane < (sl + 1) * HEAD_DIM)
        q_s = jnp.where(in_slot, q, jnp.zeros_like(q))
        s = lax.dot_general(q_s, kw, _NT_DIMS, preferred_element_type=F32)
        s = s - (slopes[sl] * dilation * LOG2E) * dist_m
        m = jnp.max(s, axis=-1, keepdims=True)
        e = jnp.exp2(s - m)
        den = jnp.sum(e, axis=-1, keepdims=True)
        o = _dot(e.astype(BF16), vw) / den
        lse = (m + jnp.log2(den)) * LN2
        o_all = jnp.where(in_slot, o, o_all)
        lse_all = jnp.where(in_slot, lse, lse_all)
    o_ref[0] = o_all.astype(o_ref.dtype)
    lse_ref[0] = lse_all


def _dilated_group(zg, group, *, qb_max):
    b, s, cols = zg.shape
    _, dilation = DIL_PAIRS[group]
    length = s // dilation
    qb = min(qb_max, length)
    win = min(length, qb + 2 * DIL_HALF)
    nblk = cols // DIL_WIDTH
    view = zg.reshape(b, length, dilation * cols)
    slopes = tuple(2.0 ** (-8.0 * (group * DIL_SLOTS + i + 1) / DIL_HEADS) for i in range(DIL_SLOTS))
    o, lse = pl.pallas_call(
        functools.partial(_dilated_kernel, dilation=dilation, slopes=slopes, qb=qb, win=win),
        out_shape=(jax.ShapeDtypeStruct((b, length, dilation * DIL_WIDTH), BF16),
                   jax.ShapeDtypeStruct((b, length, dilation * DIL_WIDTH), F32)),
        grid=(b, dilation, length // qb),
        in_specs=[
            pl.BlockSpec((1, qb, DIL_WIDTH), lambda bi, r, j: (bi, j, r * nblk)),
            pl.BlockSpec((1, length, DIL_WIDTH), lambda bi, r, j: (bi, 0, r * nblk + 1)),
            pl.BlockSpec((1, length, DIL_WIDTH), lambda bi, r, j: (bi, 0, r * nblk + 2)),
        ],
        out_specs=(pl.BlockSpec((1, qb, DIL_WIDTH), lambda bi, r, j: (bi, j, r)),
                   pl.BlockSpec((1, qb, DIL_WIDTH), lambda bi, r, j: (bi, j, r))),
        compiler_params=_cparams(("parallel", "parallel", "arbitrary")),
        name=f"dilated_attention_g{group}",
    )(view, view, view)
    return o.reshape(b, s, DIL_WIDTH), lse.reshape(b, s, DIL_WIDTH)


def _ffn_tail(x1, gf_ref, wg_ref, wu_ref, wd_ref, gfin_ref, o_ref, *, chunks, final):
    hn = _rmsnorm(x1, gf_ref[...]).astype(BF16)
    hidden = wg_ref.shape[1]
    hc = hidden // chunks
    down = None
    for c in range(chunks):
        sl = slice(c * hc, (c + 1) * hc)
        gate = _dot(hn, wg_ref[:, sl])
        up = _dot(hn, wu_ref[:, sl])
        act = gate * (1.0 / (1.0 + jnp.exp(-gate))) * up
        part = _dot(act.astype(BF16), wd_ref[sl, :])
        down = part if down is None else down + part
    y = x1 + down
    if final:
        y = _rmsnorm(y, gfin_ref[...])
    o_ref[0] = y


def _even_out_kernel(x_ref, oa_ref, o0_ref, o1_ref, o2_ref, l0_ref, l1_ref, l2_ref,
                     woa_ref, wob_ref, gf_ref, wg_ref, wu_ref, wd_ref, gfin_ref, o_ref,
                     *, chunks, final):
    l0, l1, l2 = l0_ref[0], l1_ref[0], l2_ref[0]
    mx = jnp.maximum(jnp.maximum(l0, l1), l2)
    e0, e1, e2 = jnp.exp(l0 - mx), jnp.exp(l1 - mx), jnp.exp(l2 - mx)
    tot = e0 + e1 + e2
    comb = (e0 / tot) * o0_ref[0].astype(F32) + (e1 / tot) * o1_ref[0].astype(F32) \
        + (e2 / tot) * o2_ref[0].astype(F32)
    x1 = x_ref[0] + (_dot(oa_ref[0], woa_ref[...]) + _dot(comb.astype(BF16), wob_ref[...]))
    _ffn_tail(x1, gf_ref, wg_ref, wu_ref, wd_ref, gfin_ref, o_ref, chunks=chunks, final=final)


def _odd_out_kernel(x_ref, o_in_ref, wo_ref, gf_ref, wg_ref, wu_ref, wd_ref, gfin_ref, o_ref,
                    *, chunks, final):
    x1 = x_ref[0] + _dot(o_in_ref[0], wo_ref[...])
    _ffn_tail(x1, gf_ref, wg_ref, wu_ref, wd_ref, gfin_ref, o_ref, chunks=chunks, final=final)


def _out_ffn(kernel_fn, x, acts, consts, *, ts, chunks, final):
    b, s, d = x.shape

    def row_spec(a):
        return pl.BlockSpec((1, ts, a.shape[-1]), lambda bi, si: (bi, si, 0))

    return pl.pallas_call(
        functools.partial(kernel_fn, chunks=chunks, final=final),
        out_shape=jax.ShapeDtypeStruct((b, s, d), F32),
        grid=(b, s // ts),
        in_specs=[row_spec(x)] + [row_spec(a) for a in acts] + [_const_spec(c.shape) for c in consts],
        out_specs=row_spec(x),
        compiler_params=_cparams(("parallel", "parallel")),
        name=kernel_fn.__name__.strip("_"),
    )(x, *acts, *consts)


def _rope_angles(pos, dim):
    freqs = ROPE_THETA ** (-jnp.arange(0, dim, 2, dtype=F32) / dim)
    ang = pos.astype(F32)[:, None] * freqs[None, :]
    return jnp.cos(ang), jnp.sin(ang)


def _mla_tables(s):
    cos, sin = _rope_angles(jnp.arange(s), MLA_ROPE)
    one = jnp.ones((s, MLA_NOPE), F32)
    tail = LANES - MLA_NOPE - MLA_ROPE
    cos_t = jnp.concatenate([one, cos, cos, jnp.ones((s, tail), F32)], axis=-1)
    sin_t = jnp.concatenate([jnp.zeros((s, MLA_NOPE), F32), -sin, sin, jnp.zeros((s, tail), F32)], axis=-1)
    return cos_t, sin_t


def _axial_tables(s):
    rows = s // GRID_W
    row_idx = jnp.broadcast_to(jnp.arange(rows)[:, None], (rows, GRID_W)).reshape(-1)
    col_idx = jnp.broadcast_to(jnp.arange(GRID_W)[None, :], (rows, GRID_W)).reshape(-1)
    cos_r, sin_r = _rope_angles(row_idx, HEAD_DIM // 2)
    cos_c, sin_c = _rope_angles(col_idx, HEAD_DIM // 2)
    pad = LANES - HEAD_DIM
    cos_t = jnp.concatenate([cos_r, cos_r, cos_c, cos_c, jnp.ones((s, pad), F32)], axis=-1)
    sin_t = jnp.concatenate([-sin_r, sin_r, -sin_c, sin_c, jnp.zeros((s, pad), F32)], axis=-1)
    return cos_t, sin_t


def _head_slabs(w, fill):
    kdim, heads, d = w.shape
    return jnp.pad(w, ((0, 0), (0, 0), (0, fill - d))).reshape(kdim, heads * fill).astype(BF16)


def _row(v, width=None):
    v = v.reshape(1, -1).astype(F32)
    if width is not None:
        v = jnp.pad(v, ((0, 0), (0, width - v.shape[1])))
    return v


def kernel(x, mix_norm_ab, w_in_ab, mla_q_norm, mla_kv_norm, mla_w_uq, mla_w_ukv, w_out_ab,
           mix_norm_c, gqa_w_q, gqa_w_kv, gqa_q_norm, gqa_k_norm, gqa_w_o,
           ffn_norm, ffn_w_in, ffn_w_out, final_norm):
    b, s, d = x.shape
    depth = ffn_norm.shape[0]
    hidden = ffn_w_out.shape[1]
    ts = min(512, s)
    in_a = MLA_Q_RANK + MLA_KV_RANK
    mla_cos, mla_sin = _mla_tables(s)
    ax_cos, ax_sin = _axial_tables(s)
    gfin = _row(final_norm)

    for layer in range(depth):
        i = layer // 2
        final = layer == depth - 1
        gf = _row(ffn_norm[layer])
        wg = ffn_w_in[layer][:, :hidden].astype(BF16)
        wu = ffn_w_in[layer][:, hidden:].astype(BF16)
        wd = ffn_w_out[layer].astype(BF16)
        if layer % 2 == 0:
            w_in = w_in_ab[i]
            wa = jnp.concatenate([
                w_in[:, :in_a], jnp.zeros((d, MLA_NOPE), F32), w_in[:, in_a:in_a + MLA_ROPE],
                jnp.zeros((d, LANES - MLA_NOPE - MLA_ROPE), F32)], axis=-1).astype(BF16)
            wb = w_in[:, in_a + MLA_ROPE:].astype(BF16)
            wuq = _head_slabs(mla_w_uq[i], LANES)
            wuk = _head_slabs(mla_w_ukv[i][:, :, :MLA_NOPE], LANES)
            wuv = _head_slabs(mla_w_ukv[i][:, :, MLA_NOPE:], LANES)
            q, k, v, *zgs = _even_in_proj(
                x, _row(mix_norm_ab[i]), wa, wb, _row(mla_q_norm[i]), _row(mla_kv_norm[i]),
                wuq, wuk, wuv, mla_cos, mla_sin, ts=ts)
            o_a = _attention(q, k, v, hb=2, tq=min(2048, s), tk=min(512, s), tk_fast=min(1024, s // 2),
                             bias_lane=MLA_BIAS_LANE)
            outs, lses = zip(*[_dilated_group(zgs[g], g, qb_max=256) for g in range(DIL_GROUPS)])
            n_a = MLA_HEADS * HEAD_DIM
            x = _out_ffn(_even_out_kernel, x, (o_a,) + outs + lses,
                         (w_out_ab[i][:n_a].astype(BF16), w_out_ab[i][n_a:].astype(BF16),
                          gf, wg, wu, wd, gfin), ts=ts, chunks=2, final=final)
        else:
            wq = _head_slabs(gqa_w_q[i].reshape(d, GQA_HEADS, HEAD_DIM), LANES)
            wkv = gqa_w_kv[i].reshape(d, 2, GQA_KV_HEADS, HEAD_DIM)
            wk = _head_slabs(wkv[:, 0], LANES)
            wv = _head_slabs(wkv[:, 1], LANES)
            q, k, v = _odd_in_proj(
                x, _row(mix_norm_c[i]), wq, wk, wv, _row(gqa_q_norm[i], LANES),
                _row(gqa_k_norm[i], LANES), ax_cos, ax_sin, ts=ts)
            o = _attention(q, k, v, hb=1, tq=min(512, s), tk=min(512, s), tk_fast=min(1024, s // 2),
                           bias_lane=GQA_BIAS_LANE)
            x = _out_ffn(_odd_out_kernel, x, (o,),
                         (gqa_w_o[i].astype(BF16), gf, wg, wu, wd, gfin),
                         ts=ts, chunks=2, final=final)
    return x
```

```python
import functools
import math

import jax
import jax.numpy as jnp
from jax import lax
from jax.experimental import pallas as pl
from jax.experimental.pallas import tpu as pltpu

F32 = jnp.float32
BF16 = jnp.bfloat16

NORM_EPS = 1e-6
ROPE_THETA = 10000.0
GRID_W = 64
LOG2E = math.log2(math.e)
LN2 = math.log(2.0)
MASKED = -1e30

LANES = 128
HEAD_DIM = 64

MLA_HEADS = 8
MLA_Q_RANK = 384
MLA_KV_RANK = 256
MLA_NOPE = 64
MLA_ROPE = 32

DIL_PAIRS = ((128, 1), (512, 4), (2048, 16))
DIL_HALF = 64
DIL_SLOTS = 4
DIL_GROUPS = 3
DIL_HEADS = DIL_SLOTS * DIL_GROUPS
DIL_WIDTH = DIL_SLOTS * HEAD_DIM

GQA_HEADS = 16
GQA_KV_HEADS = 4

VT_ROWS = 80

MLA_BIAS_LANE = MLA_NOPE + MLA_ROPE
GQA_BIAS_LANE = HEAD_DIM

FAST_BOUND_LIMIT = 60.0
BOUND_SLACK = 1.02

VMEM_LIMIT = 56 * 1024 * 1024


def _cparams(sem):
    return pltpu.CompilerParams(dimension_semantics=sem, vmem_limit_bytes=VMEM_LIMIT)


def _const_spec(shape):
    nd = len(shape)
    return pl.BlockSpec(shape, lambda *_: (0,) * nd, pipeline_mode=pl.Buffered(1))


def _rmsnorm(x, g):
    ms = jnp.mean(x * x, axis=-1, keepdims=True)
    return x * lax.rsqrt(ms + NORM_EPS) * g


def _dot(a, b):
    return jnp.dot(a, b, preferred_element_type=F32)


def _rope_slab(t, cos_t, sin_t, first_half):
    partner = jnp.where(first_half, pltpu.roll(t, LANES - 16, 1), pltpu.roll(t, 16, 1))
    return t * cos_t + partner * sin_t


def _first_half_mask(rows):
    lane = lax.broadcasted_iota(jnp.int32, (rows, LANES), 1)
    return (lane % 32) < 16


def _even_in_kernel(x_ref, g_ref, wa_ref, wb_ref, gq_ref, gkv_ref, wuq_ref, wuk_ref, wuv_ref,
                    cos_ref, sin_ref, q_ref, k_ref, v_ref, z0_ref, z1_ref, z2_ref, zs_ref,
                    *, q_scale, dil_scale):
    ts = x_ref.shape[1]
    xn = _rmsnorm(x_ref[0], g_ref[...]).astype(BF16)
    za = _dot(xn, wa_ref[...])
    zb = _dot(xn, wb_ref[...])
    nq = DIL_GROUPS * DIL_WIDTH
    for g, zg_ref in enumerate((z0_ref, z1_ref, z2_ref)):
        dil = DIL_PAIRS[g][1]
        lo = g * DIL_WIDTH
        parts = (zb[:, lo:lo + DIL_WIDTH] * dil_scale,
                 zb[:, nq + lo:nq + lo + DIL_WIDTH],
                 zb[:, 2 * nq + lo:2 * nq + lo + DIL_WIDTH])
        if dil == 1:
            for c, part in enumerate(parts):
                zg_ref[0, 0, :, c * DIL_WIDTH:(c + 1) * DIL_WIDTH] = part.astype(BF16)
            continue
        n_slabs = 3 * DIL_WIDTH // LANES
        for c in range(n_slabs):
            part = parts[c * LANES // DIL_WIDTH]
            off = c * LANES % DIL_WIDTH
            zs_ref[c] = part[:, off:off + LANES]
        for r in range(dil):
            for c in range(n_slabs):
                rows = zs_ref[c, pl.ds(r, ts // dil, stride=dil), :]
                zg_ref[0, r, :, c * LANES:(c + 1) * LANES] = rows.astype(BF16)

    cq = _rmsnorm(za[:, :MLA_Q_RANK], gq_ref[...]).astype(BF16)
    ckv = _rmsnorm(za[:, MLA_Q_RANK:MLA_Q_RANK + MLA_KV_RANK], gkv_ref[...]).astype(BF16)
    q_all = _dot(cq, wuq_ref[...])
    k_all = _dot(ckv, wuk_ref[...])
    v_all = _dot(ckv, wuv_ref[...])

    cos_t = cos_ref[...]
    sin_t = sin_ref[...]
    first = _first_half_mask(ts)
    lane = lax.broadcasted_iota(jnp.int32, (ts, LANES), 1)
    ones_col = (lane == HEAD_DIM).astype(F32)
    k_rope = _rope_slab(za[:, MLA_Q_RANK + MLA_KV_RANK:], cos_t, sin_t, first) \
        + (lane == MLA_BIAS_LANE).astype(F32)
    for h in range(MLA_HEADS):
        sl = slice(h * LANES, (h + 1) * LANES)
        q_ref[0, h] = _rope_slab(q_all[:, sl] * q_scale, cos_t, sin_t, first).astype(BF16)
        k_ref[0, h] = (k_all[:, sl] + k_rope).astype(BF16)
        v_ref[0, h] = (v_all[:, sl] + ones_col).T[:VT_ROWS].astype(BF16)


def _even_in_proj(x, g, wa, wb, gq, gkv, wuq, wuk, wuv, cos_t, sin_t, *, ts):
    b, s, d = x.shape
    n_s = s // ts
    head_out = jax.ShapeDtypeStruct((b, MLA_HEADS, s, LANES), BF16)
    head_spec = pl.BlockSpec((1, MLA_HEADS, ts, LANES), lambda bi, si: (bi, 0, si, 0))
    q_scale = (MLA_NOPE + MLA_ROPE) ** -0.5 * LOG2E
    dil_scale = HEAD_DIM ** -0.5 * LOG2E
    zg_outs, zg_specs = [], []
    for _, dil in DIL_PAIRS:
        zg_outs.append(jax.ShapeDtypeStruct((b, dil, s // dil, 3 * DIL_WIDTH), BF16))
        zg_specs.append(pl.BlockSpec((1, dil, ts // dil, 3 * DIL_WIDTH), lambda bi, si: (bi, 0, si, 0)))
    return pl.pallas_call(
        functools.partial(_even_in_kernel, q_scale=q_scale, dil_scale=dil_scale),
        out_shape=(head_out, head_out, jax.ShapeDtypeStruct((b, MLA_HEADS, VT_ROWS, s), BF16),
                   *zg_outs),
        grid=(b, n_s),
        in_specs=[
            pl.BlockSpec((1, ts, d), lambda bi, si: (bi, si, 0)),
            _const_spec(g.shape), _const_spec(wa.shape), _const_spec(wb.shape),
            _const_spec(gq.shape), _const_spec(gkv.shape),
            _const_spec(wuq.shape), _const_spec(wuk.shape), _const_spec(wuv.shape),
            pl.BlockSpec((ts, LANES), lambda bi, si: (si, 0)),
            pl.BlockSpec((ts, LANES), lambda bi, si: (si, 0)),
        ],
        out_specs=(head_spec, head_spec,
                   pl.BlockSpec((1, MLA_HEADS, VT_ROWS, ts), lambda bi, si: (bi, 0, 0, si)),
                   *zg_specs),
        scratch_shapes=[pltpu.VMEM((3 * DIL_WIDTH // LANES, ts, LANES), F32)],
        compiler_params=_cparams(("parallel", "parallel")),
        name="even_in_proj",
    )(x, g, wa, wb, gq, gkv, wuq, wuk, wuv, cos_t, sin_t)


def _odd_in_kernel(x_ref, g_ref, wq_ref, wk_ref, wv_ref, gq_ref, gk_ref, cos_ref, sin_ref,
                   q_ref, k_ref, v_ref, *, q_scale):
    ts = x_ref.shape[1]
    xn = _rmsnorm(x_ref[0], g_ref[...]).astype(BF16)
    q_all = _dot(xn, wq_ref[...])
    k_all = _dot(xn, wk_ref[...])
    v_all = _dot(xn, wv_ref[...])
    cos_t = cos_ref[...]
    sin_t = sin_ref[...]
    first = _first_half_mask(ts)
    lane = lax.broadcasted_iota(jnp.int32, (ts, LANES), 1)
    ones_col = (lane == HEAD_DIM).astype(F32)

    def head_norm(t, gain):
        ms = jnp.sum(t * t, axis=-1, keepdims=True) * (1.0 / HEAD_DIM)
        return t * lax.rsqrt(ms + NORM_EPS) * gain

    gq = gq_ref[...] * q_scale
    gk = gk_ref[...]
    for h in range(GQA_HEADS):
        sl = slice(h * LANES, (h + 1) * LANES)
        q_ref[0, h] = _rope_slab(head_norm(q_all[:, sl], gq), cos_t, sin_t, first).astype(BF16)
    for h in range(GQA_KV_HEADS):
        sl = slice(h * LANES, (h + 1) * LANES)
        k_ref[0, h] = (_rope_slab(head_norm(k_all[:, sl], gk), cos_t, sin_t, first) + ones_col).astype(BF16)
        v_ref[0, h] = (v_all[:, sl] + ones_col).T[:VT_ROWS].astype(BF16)


def _odd_in_proj(x, g, wq, wk, wv, gq, gk, cos_t, sin_t, *, ts):
    b, s, d = x.shape
    n_s = s // ts
    q_scale = HEAD_DIM ** -0.5 * LOG2E
    return pl.pallas_call(
        functools.partial(_odd_in_kernel, q_scale=q_scale),
        out_shape=(jax.ShapeDtypeStruct((b, GQA_HEADS, s, LANES), BF16),
                   jax.ShapeDtypeStruct((b, GQA_KV_HEADS, s, LANES), BF16),
                   jax.ShapeDtypeStruct((b, GQA_KV_HEADS, VT_ROWS, s), BF16)),
        grid=(b, n_s),
        in_specs=[
            pl.BlockSpec((1, ts, d), lambda bi, si: (bi, si, 0)),
            _const_spec(g.shape), _const_spec(wq.shape), _const_spec(wk.shape),
            _const_spec(wv.shape), _const_spec(gq.shape), _const_spec(gk.shape),
            pl.BlockSpec((ts, LANES), lambda bi, si: (si, 0)),
            pl.BlockSpec((ts, LANES), lambda bi, si: (si, 0)),
        ],
        out_specs=(pl.BlockSpec((1, GQA_HEADS, ts, LANES), lambda bi, si: (bi, 0, si, 0)),
                   pl.BlockSpec((1, GQA_KV_HEADS, ts, LANES), lambda bi, si: (bi, 0, si, 0)),
                   pl.BlockSpec((1, GQA_KV_HEADS, VT_ROWS, ts), lambda bi, si: (bi, 0, 0, si))),
        compiler_params=_cparams(("parallel", "parallel")),
        name="odd_in_proj",
    )(x, g, wq, wk, wv, gq, gk, cos_t, sin_t)


_NT_DIMS = (((1,), (1,)), ((), ()))


def _attn_kernel(q_ref, k_ref, v_ref, o_ref, kmax_scr, m_scr, acc_scr, p_scr,
                 *, hb, rep, tk, tk_fast, bias_lane, fast_limit):
    tq = q_ref.shape[2]
    s_len = k_ref.shape[2]
    m_rows = rep * tq

    @pl.when(pl.program_id(2) == 0)
    def _():
        ones = jnp.ones((LANES, LANES), BF16)
        for h in range(hb):
            def kbody(j, best, h=h):
                start = pl.multiple_of(j * tk, tk)
                kc = k_ref[0, h, pl.ds(start, tk), :].astype(F32)
                kn2 = _dot((kc * kc).astype(BF16), ones)
                return jnp.maximum(best, jnp.max(kn2, axis=0, keepdims=True))

            best = lax.fori_loop(0, s_len // tk, kbody, jnp.zeros((1, LANES), F32))
            kmax_scr[h] = jnp.broadcast_to(best, (8, LANES))

    lane = lax.broadcasted_iota(jnp.int32, (m_rows, LANES), 1)
    outs = []
    for h in range(hb):
        q2 = q_ref[0, h * rep:(h + 1) * rep].reshape(m_rows, LANES)
        qf = q2.astype(F32)
        qn2 = _dot((qf * qf).astype(BF16), jnp.ones((LANES, LANES), BF16))
        bound = qn2 * lax.rsqrt(qn2 + 1e-30) * (jnp.sqrt(kmax_scr[h][0:1, :]) * BOUND_SLACK)
        use_fast = jnp.max(bound) <= fast_limit
        acc_scr[...] = jnp.zeros((VT_ROWS, m_rows), F32)

        @pl.when(use_fast)
        def _(h=h, qf=qf, bound=bound):
            q_aug = jnp.where(lane == bias_lane, -bound, qf).astype(BF16)

            def probs(j):
                start = pl.multiple_of(j * tk_fast, tk_fast)
                kc = k_ref[0, h, pl.ds(start, tk_fast), :]
                st = lax.dot_general(kc, q_aug, _NT_DIMS, preferred_element_type=F32)
                return jnp.exp2(st).astype(BF16)

            def accumulate(j, slot):
                start = pl.multiple_of(j * tk_fast, tk_fast)
                acc_scr[...] += _dot(v_ref[0, h, :, pl.ds(start, tk_fast)], p_scr[slot])

            n_chunks = s_len // tk_fast
            p_scr[0] = probs(0)

            def body(i, carry):
                nxt = probs(2 * i + 1)
                accumulate(2 * i, 0)
                p_scr[1] = nxt
                nxt = probs(2 * i + 2)
                accumulate(2 * i + 1, 1)
                p_scr[0] = nxt
                return carry

            lax.fori_loop(0, n_chunks // 2 - 1, body, 0)
            nxt = probs(n_chunks - 1)
            accumulate(n_chunks - 2, 0)
            p_scr[1] = nxt
            accumulate(n_chunks - 1, 1)

        @pl.when(jnp.logical_not(use_fast))
        def _(h=h, q2=q2):
            m_scr[...] = jnp.full((1, m_rows), MASKED, F32)

            def body(j, carry):
                start = pl.multiple_of(j * tk, tk)
                kc = k_ref[0, h, pl.ds(start, tk), :]
                vtc = v_ref[0, h, :, pl.ds(start, tk)]
                st = lax.dot_general(kc, q2, _NT_DIMS, preferred_element_type=F32)
                m_old = m_scr[...]
                m_new = jnp.maximum(m_old, jnp.max(st, axis=0, keepdims=True))
                p = jnp.exp2(st - m_new)
                alpha = jnp.exp2(m_old - m_new)
                acc_scr[...] = alpha * acc_scr[...] + _dot(vtc, p.astype(BF16))
                m_scr[...] = m_new
                return carry

            lax.fori_loop(0, s_len // tk, body, 0)

        acc = acc_scr[...]
        o = (acc[:HEAD_DIM] / acc[HEAD_DIM:HEAD_DIM + 1]).T
        for r in range(rep):
            outs.append(o[r * tq:(r + 1) * tq])
    o_ref[0] = jnp.concatenate(outs, axis=-1).astype(o_ref.dtype)


def _attention(q, k, v, *, hb, tq, tk, tk_fast, bias_lane, fast_limit=FAST_BOUND_LIMIT):
    b, hq, s, _ = q.shape
    hk = k.shape[1]
    rep = hq // hk
    width = hb * rep * HEAD_DIM
    m_rows = rep * tq
    assert (s // tk_fast) % 2 == 0, "the two-slot pipeline consumes key chunks in pairs"
    return pl.pallas_call(
        functools.partial(_attn_kernel, hb=hb, rep=rep, tk=tk, tk_fast=tk_fast,
                          bias_lane=bias_lane, fast_limit=fast_limit),
        out_shape=jax.ShapeDtypeStruct((b, s, hq * HEAD_DIM), BF16),
        grid=(b, hk // hb, s // tq),
        in_specs=[
            pl.BlockSpec((1, hb * rep, tq, LANES), lambda bi, gi, qi: (bi, gi, qi, 0)),
            pl.BlockSpec((1, hb, s, LANES), lambda bi, gi, qi: (bi, gi, 0, 0)),
            pl.BlockSpec((1, hb, VT_ROWS, s), lambda bi, gi, qi: (bi, gi, 0, 0)),
        ],
        out_specs=pl.BlockSpec((1, tq, width), lambda bi, gi, qi: (bi, qi, gi)),
        scratch_shapes=[pltpu.VMEM((hb, 8, LANES), F32), pltpu.VMEM((1, m_rows), F32),
                        pltpu.VMEM((VT_ROWS, m_rows), F32), pltpu.VMEM((2, tk_fast, m_rows), BF16)],
        compiler_params=_cparams(("parallel", "parallel", "arbitrary")),
        name="dense_attention",
    )(q, k, v)


def _dilated_kernel(q_ref, k_ref, v_ref, o_ref, lse_ref, *, dilation, slopes, qb, win):
    length = k_ref.shape[2]
    u0 = pl.program_id(2) * qb
    start = jnp.clip(u0 - DIL_HALF, 0, length - win)
    start = pl.multiple_of(start, DIL_HALF)
    kw = k_ref[0, 0, pl.ds(start, win), :]
    vw = v_ref[0, 0, pl.ds(start, win), :]
    q = q_ref[0, 0]
    row = lax.broadcasted_iota(jnp.int32, (qb, win), 0)
    col = lax.broadcasted_iota(jnp.int32, (qb, win), 1)
    rel = col - row + (start - u0)
    dist = jnp.abs(rel)
    dist_m = jnp.where(dist <= DIL_HALF, dist.astype(F32), -MASKED)
    lane = lax.broadcasted_iota(jnp.int32, (qb, DIL_WIDTH), 1)
    o_all = jnp.zeros((qb, DIL_WIDTH), F32)
    lse_all = jnp.zeros((qb, DIL_WIDTH), F32)
    for sl in range(DIL_SLOTS):
        in_slot = (lane >= sl * HEAD_DIM) & (lane < (sl + 1) * HEAD_DIM)
        q_s = jnp.where(in_slot, q, jnp.zeros_like(q))
        s = lax.dot_general(q_s, kw, _NT_DIMS, preferred_element_type=F32)
        s = s - (slopes[sl] * dilation * LOG2E) * dist_m
        m = jnp.max(s, axis=-1, keepdims=True)
        e = jnp.exp2(s - m)
        den = jnp.sum(e, axis=-1, keepdims=True)
        o = _dot(e.astype(BF16), vw) / den
        lse = (m + jnp.log2(den)) * LN2
        o_all = jnp.where(in_slot, o, o_all)
        lse_all = jnp.where(in_slot, lse, lse_all)
    o_ref[0, 0] = o_all.astype(o_ref.dtype)
    lse_ref[0, 0] = lse_all


def _dilated_group(zg, group, *, qb_max):
    b, dilation, length, _ = zg.shape
    assert dilation == DIL_PAIRS[group][1]
    qb = min(qb_max, length)
    win = min(length, qb + 2 * DIL_HALF)
    slopes = tuple(2.0 ** (-8.0 * (group * DIL_SLOTS + i + 1) / DIL_HEADS) for i in range(DIL_SLOTS))
    out_spec = pl.BlockSpec((1, 1, qb, DIL_WIDTH), lambda bi, r, j: (bi, r, j, 0))
    return pl.pallas_call(
        functools.partial(_dilated_kernel, dilation=dilation, slopes=slopes, qb=qb, win=win),
        out_shape=(jax.ShapeDtypeStruct((b, dilation, length, DIL_WIDTH), BF16),
                   jax.ShapeDtypeStruct((b, dilation, length, DIL_WIDTH), F32)),
        grid=(b, dilation, length // qb),
        in_specs=[
            pl.BlockSpec((1, 1, qb, DIL_WIDTH), lambda bi, r, j: (bi, r, j, 0)),
            pl.BlockSpec((1, 1, length, DIL_WIDTH), lambda bi, r, j: (bi, r, 0, 1)),
            pl.BlockSpec((1, 1, length, DIL_WIDTH), lambda bi, r, j: (bi, r, 0, 2)),
        ],
        out_specs=(out_spec, out_spec),
        compiler_params=_cparams(("parallel", "parallel", "arbitrary")),
        name=f"dilated_attention_g{group}",
    )(zg, zg, zg)


def _ffn_tail(x1, gf_ref, wg_ref, wu_ref, wd_ref, gfin_ref, o_ref, *, chunks, final):
    hn = _rmsnorm(x1, gf_ref[...]).astype(BF16)
    hidden = wg_ref.shape[1]
    hc = hidden // chunks
    down = None
    for c in range(chunks):
        sl = slice(c * hc, (c + 1) * hc)
        gate = _dot(hn, wg_ref[:, sl])
        up = _dot(hn, wu_ref[:, sl])
        act = gate * (1.0 / (1.0 + jnp.exp(-gate))) * up
        part = _dot(act.astype(BF16), wd_ref[sl, :])
        down = part if down is None else down + part
    y = x1 + down
    if final:
        y = _rmsnorm(y, gfin_ref[...])
    o_ref[0] = y


def _token_major(ref, scr):
    dil, per_class = ref.shape[1], ref.shape[2]
    if dil == 1:
        return ref[0, 0].astype(F32)
    for r in range(dil):
        blk = ref[0, r].astype(F32)
        for c in range(DIL_WIDTH // LANES):
            scr[c, pl.ds(r, per_class, stride=dil), :] = blk[:, c * LANES:(c + 1) * LANES]
    return jnp.concatenate([scr[c] for c in range(DIL_WIDTH // LANES)], axis=-1)


def _even_out_kernel(x_ref, oa_ref, o0_ref, o1_ref, o2_ref, l0_ref, l1_ref, l2_ref,
                     woa_ref, wob_ref, gf_ref, wg_ref, wu_ref, wd_ref, gfin_ref, o_ref,
                     so1_scr, so2_scr, sl1_scr, sl2_scr, *, chunks, final):
    o0, o1, o2 = _token_major(o0_ref, None), _token_major(o1_ref, so1_scr), _token_major(o2_ref, so2_scr)
    l0, l1, l2 = _token_major(l0_ref, None), _token_major(l1_ref, sl1_scr), _token_major(l2_ref, sl2_scr)
    mx = jnp.maximum(jnp.maximum(l0, l1), l2)
    e0, e1, e2 = jnp.exp(l0 - mx), jnp.exp(l1 - mx), jnp.exp(l2 - mx)
    tot = e0 + e1 + e2
    comb = (e0 / tot) * o0 + (e1 / tot) * o1 + (e2 / tot) * o2
    x1 = x_ref[0] + (_dot(oa_ref[0], woa_ref[...]) + _dot(comb.astype(BF16), wob_ref[...]))
    _ffn_tail(x1, gf_ref, wg_ref, wu_ref, wd_ref, gfin_ref, o_ref, chunks=chunks, final=final)


def _odd_out_kernel(x_ref, o_in_ref, wo_ref, gf_ref, wg_ref, wu_ref, wd_ref, gfin_ref, o_ref,
                    *, chunks, final):
    x1 = x_ref[0] + _dot(o_in_ref[0], wo_ref[...])
    _ffn_tail(x1, gf_ref, wg_ref, wu_ref, wd_ref, gfin_ref, o_ref, chunks=chunks, final=final)


def _out_ffn(kernel_fn, x, acts, consts, *, ts, chunks, final, scratch=()):
    b, s, d = x.shape

    def row_spec(a):
        if a.ndim == 4:
            dil = a.shape[1]
            return pl.BlockSpec((1, dil, ts // dil, a.shape[-1]), lambda bi, si: (bi, 0, si, 0))
        return pl.BlockSpec((1, ts, a.shape[-1]), lambda bi, si: (bi, si, 0))

    return pl.pallas_call(
        functools.partial(kernel_fn, chunks=chunks, final=final),
        out_shape=jax.ShapeDtypeStruct((b, s, d), F32),
        grid=(b, s // ts),
        in_specs=[row_spec(x)] + [row_spec(a) for a in acts] + [_const_spec(c.shape) for c in consts],
        out_specs=row_spec(x),
        scratch_shapes=list(scratch),
        compiler_params=_cparams(("parallel", "parallel")),
        name=kernel_fn.__name__.strip("_"),
    )(x, *acts, *consts)


def _rope_angles(pos, dim):
    freqs = ROPE_THETA ** (-jnp.arange(0, dim, 2, dtype=F32) / dim)
    ang = pos.astype(F32)[:, None] * freqs[None, :]
    return jnp.cos(ang), jnp.sin(ang)


def _mla_tables(s):
    cos, sin = _rope_angles(jnp.arange(s), MLA_ROPE)
    one = jnp.ones((s, MLA_NOPE), F32)
    tail = LANES - MLA_NOPE - MLA_ROPE
    cos_t = jnp.concatenate([one, cos, cos, jnp.ones((s, tail), F32)], axis=-1)
    sin_t = jnp.concatenate([jnp.zeros((s, MLA_NOPE), F32), -sin, sin, jnp.zeros((s, tail), F32)], axis=-1)
    return cos_t, sin_t


def _axial_tables(s):
    rows = s // GRID_W
    row_idx = jnp.broadcast_to(jnp.arange(rows)[:, None], (rows, GRID_W)).reshape(-1)
    col_idx = jnp.broadcast_to(jnp.arange(GRID_W)[None, :], (rows, GRID_W)).reshape(-1)
    cos_r, sin_r = _rope_angles(row_idx, HEAD_DIM // 2)
    cos_c, sin_c = _rope_angles(col_idx, HEAD_DIM // 2)
    pad = LANES - HEAD_DIM
    cos_t = jnp.concatenate([cos_r, cos_r, cos_c, cos_c, jnp.ones((s, pad), F32)], axis=-1)
    sin_t = jnp.concatenate([-sin_r, sin_r, -sin_c, sin_c, jnp.zeros((s, pad), F32)], axis=-1)
    return cos_t, sin_t


def _head_slabs(w, fill):
    kdim, heads, d = w.shape
    return jnp.pad(w, ((0, 0), (0, 0), (0, fill - d))).reshape(kdim, heads * fill).astype(BF16)


def _row(v, width=None):
    v = v.reshape(1, -1).astype(F32)
    if width is not None:
        v = jnp.pad(v, ((0, 0), (0, width - v.shape[1])))
    return v


def kernel(x, mix_norm_ab, w_in_ab, mla_q_norm, mla_kv_norm, mla_w_uq, mla_w_ukv, w_out_ab,
           mix_norm_c, gqa_w_q, gqa_w_kv, gqa_q_norm, gqa_k_norm, gqa_w_o,
           ffn_norm, ffn_w_in, ffn_w_out, final_norm):
    b, s, d = x.shape
    depth = ffn_norm.shape[0]
    hidden = ffn_w_out.shape[1]
    ts = min(512, s)
    in_a = MLA_Q_RANK + MLA_KV_RANK
    mla_cos, mla_sin = _mla_tables(s)
    ax_cos, ax_sin = _axial_tables(s)
    gfin = _row(final_norm)

    for layer in range(depth):
        i = layer // 2
        final = layer == depth - 1
        gf = _row(ffn_norm[layer])
        wg = ffn_w_in[layer][:, :hidden].astype(BF16)
        wu = ffn_w_in[layer][:, hidden:].astype(BF16)
        wd = ffn_w_out[layer].astype(BF16)
        if layer % 2 == 0:
            w_in = w_in_ab[i]
            wa = jnp.concatenate([
                w_in[:, :in_a], jnp.zeros((d, MLA_NOPE), F32), w_in[:, in_a:in_a + MLA_ROPE],
                jnp.zeros((d, LANES - MLA_NOPE - MLA_ROPE), F32)], axis=-1).astype(BF16)
            wb = w_in[:, in_a + MLA_ROPE:].astype(BF16)
            wuq = _head_slabs(mla_w_uq[i], LANES)
            wuk = _head_slabs(mla_w_ukv[i][:, :, :MLA_NOPE], LANES)
            wuv = _head_slabs(mla_w_ukv[i][:, :, MLA_NOPE:], LANES)
            q, k, v, *zgs = _even_in_proj(
                x, _row(mix_norm_ab[i]), wa, wb, _row(mla_q_norm[i]), _row(mla_kv_norm[i]),
                wuq, wuk, wuv, mla_cos, mla_sin, ts=ts)
            o_a = _attention(q, k, v, hb=2, tq=min(2048, s), tk=min(512, s), tk_fast=min(1024, s // 2),
                             bias_lane=MLA_BIAS_LANE)
            outs, lses = zip(*[_dilated_group(zgs[g], g, qb_max=256) for g in range(DIL_GROUPS)])
            n_a = MLA_HEADS * HEAD_DIM
            x = _out_ffn(_even_out_kernel, x, (o_a,) + outs + lses,
                         (w_out_ab[i][:n_a].astype(BF16), w_out_ab[i][n_a:].astype(BF16),
                          gf, wg, wu, wd, gfin), ts=ts, chunks=2, final=final,
                         scratch=[pltpu.VMEM((DIL_WIDTH // LANES, ts, LANES), F32)] * 4)
        else:
            wq = _head_slabs(gqa_w_q[i].reshape(d, GQA_HEADS, HEAD_DIM), LANES)
            wkv = gqa_w_kv[i].reshape(d, 2, GQA_KV_HEADS, HEAD_DIM)
            wk = _head_slabs(wkv[:, 0], LANES)
            wv = _head_slabs(wkv[:, 1], LANES)
            q, k, v = _odd_in_proj(
                x, _row(mix_norm_c[i]), wq, wk, wv, _row(gqa_q_norm[i], LANES),
                _row(gqa_k_norm[i], LANES), ax_cos, ax_sin, ts=ts)
            o = _attention(q, k, v, hb=1, tq=min(512, s), tk=min(512, s), tk_fast=min(1024, s // 2),
                           bias_lane=GQA_BIAS_LANE)
            x = _out_ffn(_odd_out_kernel, x, (o,),
                         (gqa_w_o[i].astype(BF16), gf, wg, wu, wd, gfin),
                         ts=ts, chunks=2, final=final)
    return x
```

```python
import functools
import math
from typing import NamedTuple

import jax
import jax.numpy as jnp
from jax import lax
from jax.experimental import pallas as pl
from jax.experimental.pallas import tpu as pltpu

F32 = jnp.float32
BF16 = jnp.bfloat16

NORM_EPS = 1e-6
ROPE_THETA = 10000.0
GRID_W = 64
LOG2E = math.log2(math.e)
LN2 = math.log(2.0)
MASKED = -1e30

LANES = 128
SUBLANES = 8
ROPE_HALF = 16
TINY = 1e-30
HEAD_DIM = 64

MLA_HEADS = 8
MLA_Q_RANK = 384
MLA_KV_RANK = 256
MLA_NOPE = 64
MLA_ROPE = 32

DIL_PAIRS = ((128, 1), (512, 4), (2048, 16))
DIL_HALF = 64
DIL_SLOTS = 4
DIL_GROUPS = 3
DIL_HEADS = DIL_SLOTS * DIL_GROUPS
DIL_WIDTH = DIL_SLOTS * HEAD_DIM

GQA_HEADS = 16
GQA_KV_HEADS = 4

VT_ROWS = 80
MLA_BIAS_LANE = MLA_NOPE + MLA_ROPE
GQA_BIAS_LANE = HEAD_DIM

FAST_BOUND_LIMIT = 60.0
BOUND_SLACK = 1.02

VMEM_LIMIT = 56 * 1024 * 1024


def _cparams(sem):
    return pltpu.CompilerParams(dimension_semantics=sem, vmem_limit_bytes=VMEM_LIMIT)


def _const_spec(shape):
    nd = len(shape)
    return pl.BlockSpec(shape, lambda *_: (0,) * nd, pipeline_mode=pl.Buffered(1))


def _rmsnorm(x, g):
    ms = jnp.mean(x * x, axis=-1, keepdims=True)
    return x * lax.rsqrt(ms + NORM_EPS) * g


def _dot(a, b):
    return jnp.dot(a, b, preferred_element_type=F32)


def _rope_slab(t, cos_t, sin_t, first_half):
    partner = jnp.where(first_half, pltpu.roll(t, LANES - ROPE_HALF, 1), pltpu.roll(t, ROPE_HALF, 1))
    return t * cos_t + partner * sin_t


def _first_half_mask(rows):
    lane = lax.broadcasted_iota(jnp.int32, (rows, LANES), 1)
    return (lane % (2 * ROPE_HALF)) < ROPE_HALF


def _even_in_kernel(x_ref, g_ref, wa_ref, wb_ref, gq_ref, gkv_ref, wuq_ref, wuk_ref, wuv_ref,
                    cos_ref, sin_ref, q_ref, k_ref, v_ref, z0_ref, z1_ref, z2_ref, zs_ref,
                    *, q_scale, dil_scale):
    ts = x_ref.shape[1]
    xn = _rmsnorm(x_ref[0], g_ref[...]).astype(BF16)
    za = _dot(xn, wa_ref[...])
    zb = _dot(xn, wb_ref[...])
    nq = DIL_GROUPS * DIL_WIDTH
    for g, zg_ref in enumerate((z0_ref, z1_ref, z2_ref)):
        dil = DIL_PAIRS[g][1]
        lo = g * DIL_WIDTH
        parts = (zb[:, lo:lo + DIL_WIDTH] * dil_scale,
                 zb[:, nq + lo:nq + lo + DIL_WIDTH],
                 zb[:, 2 * nq + lo:2 * nq + lo + DIL_WIDTH])
        if dil == 1:
            for c, part in enumerate(parts):
                zg_ref[0, 0, :, c * DIL_WIDTH:(c + 1) * DIL_WIDTH] = part.astype(BF16)
            continue
        n_slabs = 3 * DIL_WIDTH // LANES
        for c in range(n_slabs):
            part = parts[c * LANES // DIL_WIDTH]
            off = c * LANES % DIL_WIDTH
            zs_ref[c] = part[:, off:off + LANES]
        for r in range(dil):
            for c in range(n_slabs):
                rows = zs_ref[c, pl.ds(r, ts // dil, stride=dil), :]
                zg_ref[0, r, :, c * LANES:(c + 1) * LANES] = rows.astype(BF16)

    cq = _rmsnorm(za[:, :MLA_Q_RANK], gq_ref[...]).astype(BF16)
    ckv = _rmsnorm(za[:, MLA_Q_RANK:MLA_Q_RANK + MLA_KV_RANK], gkv_ref[...]).astype(BF16)
    q_all = _dot(cq, wuq_ref[...])
    k_all = _dot(ckv, wuk_ref[...])
    v_all = _dot(ckv, wuv_ref[...])

    cos_t = cos_ref[...]
    sin_t = sin_ref[...]
    first = _first_half_mask(ts)
    lane = lax.broadcasted_iota(jnp.int32, (ts, LANES), 1)
    ones_col = (lane == HEAD_DIM).astype(F32)
    k_rope = _rope_slab(za[:, MLA_Q_RANK + MLA_KV_RANK:], cos_t, sin_t, first) \
        + (lane == MLA_BIAS_LANE).astype(F32)
    for h in range(MLA_HEADS):
        sl = slice(h * LANES, (h + 1) * LANES)
        q_ref[0, h] = _rope_slab(q_all[:, sl] * q_scale, cos_t, sin_t, first).astype(BF16)
        k_ref[0, h] = (k_all[:, sl] + k_rope).astype(BF16)
        v_ref[0, h] = (v_all[:, sl] + ones_col).T[:VT_ROWS].astype(BF16)


def _even_in_proj(x, g, wa, wb, gq, gkv, wuq, wuk, wuv, cos_t, sin_t, *, ts):
    b, s, d = x.shape
    n_s = s // ts
    head_out = jax.ShapeDtypeStruct((b, MLA_HEADS, s, LANES), BF16)
    head_spec = pl.BlockSpec((1, MLA_HEADS, ts, LANES), lambda bi, si: (bi, 0, si, 0))
    q_scale = (MLA_NOPE + MLA_ROPE) ** -0.5 * LOG2E
    dil_scale = HEAD_DIM ** -0.5 * LOG2E
    zg_outs, zg_specs = [], []
    for _, dil in DIL_PAIRS:
        zg_outs.append(jax.ShapeDtypeStruct((b, dil, s // dil, 3 * DIL_WIDTH), BF16))
        zg_specs.append(pl.BlockSpec((1, dil, ts // dil, 3 * DIL_WIDTH), lambda bi, si: (bi, 0, si, 0)))
    return pl.pallas_call(
        functools.partial(_even_in_kernel, q_scale=q_scale, dil_scale=dil_scale),
        out_shape=(head_out, head_out, jax.ShapeDtypeStruct((b, MLA_HEADS, VT_ROWS, s), BF16),
                   *zg_outs),
        grid=(b, n_s),
        in_specs=[
            pl.BlockSpec((1, ts, d), lambda bi, si: (bi, si, 0)),
            _const_spec(g.shape), _const_spec(wa.shape), _const_spec(wb.shape),
            _const_spec(gq.shape), _const_spec(gkv.shape),
            _const_spec(wuq.shape), _const_spec(wuk.shape), _const_spec(wuv.shape),
            pl.BlockSpec((ts, LANES), lambda bi, si: (si, 0)),
            pl.BlockSpec((ts, LANES), lambda bi, si: (si, 0)),
        ],
        out_specs=(head_spec, head_spec,
                   pl.BlockSpec((1, MLA_HEADS, VT_ROWS, ts), lambda bi, si: (bi, 0, 0, si)),
                   *zg_specs),
        scratch_shapes=[pltpu.VMEM((3 * DIL_WIDTH // LANES, ts, LANES), F32)],
        compiler_params=_cparams(("parallel", "parallel")),
        name="even_in_proj",
    )(x, g, wa, wb, gq, gkv, wuq, wuk, wuv, cos_t, sin_t)


def _odd_in_kernel(x_ref, g_ref, wq_ref, wk_ref, wv_ref, gq_ref, gk_ref, cos_ref, sin_ref,
                   q_ref, k_ref, v_ref, *, q_scale):
    ts = x_ref.shape[1]
    xn = _rmsnorm(x_ref[0], g_ref[...]).astype(BF16)
    q_all = _dot(xn, wq_ref[...])
    k_all = _dot(xn, wk_ref[...])
    v_all = _dot(xn, wv_ref[...])
    cos_t = cos_ref[...]
    sin_t = sin_ref[...]
    first = _first_half_mask(ts)
    lane = lax.broadcasted_iota(jnp.int32, (ts, LANES), 1)
    ones_col = (lane == HEAD_DIM).astype(F32)

    def head_norm(t, gain):
        ms = jnp.sum(t * t, axis=-1, keepdims=True) * (1.0 / HEAD_DIM)
        return t * lax.rsqrt(ms + NORM_EPS) * gain

    gq = gq_ref[...] * q_scale
    gk = gk_ref[...]
    for h in range(GQA_HEADS):
        sl = slice(h * LANES, (h + 1) * LANES)
        q_ref[0, h] = _rope_slab(head_norm(q_all[:, sl], gq), cos_t, sin_t, first).astype(BF16)
    for h in range(GQA_KV_HEADS):
        sl = slice(h * LANES, (h + 1) * LANES)
        k_ref[0, h] = (_rope_slab(head_norm(k_all[:, sl], gk), cos_t, sin_t, first) + ones_col).astype(BF16)
        v_ref[0, h] = (v_all[:, sl] + ones_col).T[:VT_ROWS].astype(BF16)


def _odd_in_proj(x, g, wq, wk, wv, gq, gk, cos_t, sin_t, *, ts):
    b, s, d = x.shape
    n_s = s // ts
    q_scale = HEAD_DIM ** -0.5 * LOG2E
    return pl.pallas_call(
        functools.partial(_odd_in_kernel, q_scale=q_scale),
        out_shape=(jax.ShapeDtypeStruct((b, GQA_HEADS, s, LANES), BF16),
                   jax.ShapeDtypeStruct((b, GQA_KV_HEADS, s, LANES), BF16),
                   jax.ShapeDtypeStruct((b, GQA_KV_HEADS, VT_ROWS, s), BF16)),
        grid=(b, n_s),
        in_specs=[
            pl.BlockSpec((1, ts, d), lambda bi, si: (bi, si, 0)),
            _const_spec(g.shape), _const_spec(wq.shape), _const_spec(wk.shape),
            _const_spec(wv.shape), _const_spec(gq.shape), _const_spec(gk.shape),
            pl.BlockSpec((ts, LANES), lambda bi, si: (si, 0)),
            pl.BlockSpec((ts, LANES), lambda bi, si: (si, 0)),
        ],
        out_specs=(pl.BlockSpec((1, GQA_HEADS, ts, LANES), lambda bi, si: (bi, 0, si, 0)),
                   pl.BlockSpec((1, GQA_KV_HEADS, ts, LANES), lambda bi, si: (bi, 0, si, 0)),
                   pl.BlockSpec((1, GQA_KV_HEADS, VT_ROWS, ts), lambda bi, si: (bi, 0, 0, si))),
        compiler_params=_cparams(("parallel", "parallel")),
        name="odd_in_proj",
    )(x, g, wq, wk, wv, gq, gk, cos_t, sin_t)


_NT_DIMS = (((1,), (1,)), ((), ()))


def _attn_kernel(q_ref, k_ref, v_ref, o_ref, kmax_scr, m_scr, acc_scr, p_scr,
                 *, hb, rep, tk, tk_fast, bias_lane, fast_limit):
    tq = q_ref.shape[2]
    s_len = k_ref.shape[2]
    m_rows = rep * tq

    @pl.when(pl.program_id(2) == 0)
    def _():
        ones = jnp.ones((LANES, LANES), BF16)
        for h in range(hb):
            def kbody(j, best, h=h):
                start = pl.multiple_of(j * tk, tk)
                kc = k_ref[0, h, pl.ds(start, tk), :].astype(F32)
                kn2 = _dot((kc * kc).astype(BF16), ones)
                return jnp.maximum(best, jnp.max(kn2, axis=0, keepdims=True))

            best = lax.fori_loop(0, s_len // tk, kbody, jnp.zeros((1, LANES), F32))
            kmax_scr[h] = jnp.broadcast_to(best, (SUBLANES, LANES))

    lane = lax.broadcasted_iota(jnp.int32, (m_rows, LANES), 1)
    outs = []
    for h in range(hb):
        q2 = q_ref[0, h * rep:(h + 1) * rep].reshape(m_rows, LANES)
        qf = q2.astype(F32)
        qn2 = _dot((qf * qf).astype(BF16), jnp.ones((LANES, LANES), BF16))
        bound = qn2 * lax.rsqrt(qn2 + TINY) * (jnp.sqrt(kmax_scr[h][0:1, :]) * BOUND_SLACK)
        use_fast = jnp.max(bound) <= fast_limit
        acc_scr[...] = jnp.zeros((VT_ROWS, m_rows), F32)

        @pl.when(use_fast)
        def _(h=h, qf=qf, bound=bound):
            q_aug = jnp.where(lane == bias_lane, -bound, qf).astype(BF16)

            def probs(j):
                start = pl.multiple_of(j * tk_fast, tk_fast)
                kc = k_ref[0, h, pl.ds(start, tk_fast), :]
                st = lax.dot_general(kc, q_aug, _NT_DIMS, preferred_element_type=F32)
                return jnp.exp2(st).astype(BF16)

            def accumulate(j, slot):
                start = pl.multiple_of(j * tk_fast, tk_fast)
                acc_scr[...] += _dot(v_ref[0, h, :, pl.ds(start, tk_fast)], p_scr[slot])

            n_chunks = s_len // tk_fast
            p_scr[0] = probs(0)

            def body(i, carry):
                nxt = probs(2 * i + 1)
                accumulate(2 * i, 0)
                p_scr[1] = nxt
                nxt = probs(2 * i + 2)
                accumulate(2 * i + 1, 1)
                p_scr[0] = nxt
                return carry

            lax.fori_loop(0, n_chunks // 2 - 1, body, 0)
            nxt = probs(n_chunks - 1)
            accumulate(n_chunks - 2, 0)
            p_scr[1] = nxt
            accumulate(n_chunks - 1, 1)

        @pl.when(jnp.logical_not(use_fast))
        def _(h=h, q2=q2):
            m_scr[...] = jnp.full((1, m_rows), MASKED, F32)

            def body(j, carry):
                start = pl.multiple_of(j * tk, tk)
                kc = k_ref[0, h, pl.ds(start, tk), :]
                vtc = v_ref[0, h, :, pl.ds(start, tk)]
                st = lax.dot_general(kc, q2, _NT_DIMS, preferred_element_type=F32)
                m_old = m_scr[...]
                m_new = jnp.maximum(m_old, jnp.max(st, axis=0, keepdims=True))
                p = jnp.exp2(st - m_new)
                alpha = jnp.exp2(m_old - m_new)
                acc_scr[...] = alpha * acc_scr[...] + _dot(vtc, p.astype(BF16))
                m_scr[...] = m_new
                return carry

            lax.fori_loop(0, s_len // tk, body, 0)

        acc = acc_scr[...]
        o = (acc[:HEAD_DIM] / acc[HEAD_DIM:HEAD_DIM + 1]).T
        for r in range(rep):
            outs.append(o[r * tq:(r + 1) * tq])
    o_ref[0] = jnp.concatenate(outs, axis=-1).astype(o_ref.dtype)


def _attention(q, k, v, *, hb, tq, tk, tk_fast, bias_lane, fast_limit=FAST_BOUND_LIMIT):
    b, hq, s, _ = q.shape
    hk = k.shape[1]
    rep = hq // hk
    width = hb * rep * HEAD_DIM
    m_rows = rep * tq
    assert (s // tk_fast) % 2 == 0, "the two-slot pipeline consumes key chunks in pairs"
    return pl.pallas_call(
        functools.partial(_attn_kernel, hb=hb, rep=rep, tk=tk, tk_fast=tk_fast,
                          bias_lane=bias_lane, fast_limit=fast_limit),
        out_shape=jax.ShapeDtypeStruct((b, s, hq * HEAD_DIM), BF16),
        grid=(b, hk // hb, s // tq),
        in_specs=[
            pl.BlockSpec((1, hb * rep, tq, LANES), lambda bi, gi, qi: (bi, gi, qi, 0)),
            pl.BlockSpec((1, hb, s, LANES), lambda bi, gi, qi: (bi, gi, 0, 0)),
            pl.BlockSpec((1, hb, VT_ROWS, s), lambda bi, gi, qi: (bi, gi, 0, 0)),
        ],
        out_specs=pl.BlockSpec((1, tq, width), lambda bi, gi, qi: (bi, qi, gi)),
        scratch_shapes=[pltpu.VMEM((hb, SUBLANES, LANES), F32), pltpu.VMEM((1, m_rows), F32),
                        pltpu.VMEM((VT_ROWS, m_rows), F32), pltpu.VMEM((2, tk_fast, m_rows), BF16)],
        compiler_params=_cparams(("parallel", "parallel", "arbitrary")),
        name="dense_attention",
    )(q, k, v)


def _dilated_kernel(q_ref, k_ref, v_ref, o_ref, lse_ref, *, dilation, slopes, qb, win):
    length = k_ref.shape[2]
    u0 = pl.program_id(2) * qb
    start = jnp.clip(u0 - DIL_HALF, 0, length - win)
    start = pl.multiple_of(start, DIL_HALF)
    kw = k_ref[0, 0, pl.ds(start, win), :]
    vw = v_ref[0, 0, pl.ds(start, win), :]
    q = q_ref[0, 0]
    row = lax.broadcasted_iota(jnp.int32, (qb, win), 0)
    col = lax.broadcasted_iota(jnp.int32, (qb, win), 1)
    rel = col - row + (start - u0)
    dist = jnp.abs(rel)
    dist_m = jnp.where(dist <= DIL_HALF, dist.astype(F32), -MASKED)
    lane = lax.broadcasted_iota(jnp.int32, (qb, DIL_WIDTH), 1)
    o_all = jnp.zeros((qb, DIL_WIDTH), F32)
    lse_all = jnp.zeros((qb, DIL_WIDTH), F32)
    for sl in range(DIL_SLOTS):
        in_slot = (lane >= sl * HEAD_DIM) & (lane < (sl + 1) * HEAD_DIM)
        q_s = jnp.where(in_slot, q, jnp.zeros_like(q))
        s = lax.dot_general(q_s, kw, _NT_DIMS, preferred_element_type=F32)
        s = s - (slopes[sl] * dilation * LOG2E) * dist_m
        m = jnp.max(s, axis=-1, keepdims=True)
        e = jnp.exp2(s - m)
        den = jnp.sum(e, axis=-1, keepdims=True)
        o = _dot(e.astype(BF16), vw) / den
        lse = (m + jnp.log2(den)) * LN2
        o_all = jnp.where(in_slot, o, o_all)
        lse_all = jnp.where(in_slot, lse, lse_all)
    o_ref[0, 0] = o_all.astype(o_ref.dtype)
    lse_ref[0, 0] = lse_all


def _dilated_group(zg, group, *, qb_max):
    b, dilation, length, _ = zg.shape
    assert dilation == DIL_PAIRS[group][1]
    qb = min(qb_max, length)
    win = min(length, qb + 2 * DIL_HALF)
    slopes = tuple(2.0 ** (-8.0 * (group * DIL_SLOTS + i + 1) / DIL_HEADS) for i in range(DIL_SLOTS))
    out_spec = pl.BlockSpec((1, 1, qb, DIL_WIDTH), lambda bi, r, j: (bi, r, j, 0))
    return pl.pallas_call(
        functools.partial(_dilated_kernel, dilation=dilation, slopes=slopes, qb=qb, win=win),
        out_shape=(jax.ShapeDtypeStruct((b, dilation, length, DIL_WIDTH), BF16),
                   jax.ShapeDtypeStruct((b, dilation, length, DIL_WIDTH), F32)),
        grid=(b, dilation, length // qb),
        in_specs=[
            pl.BlockSpec((1, 1, qb, DIL_WIDTH), lambda bi, r, j: (bi, r, j, 0)),
            pl.BlockSpec((1, 1, length, DIL_WIDTH), lambda bi, r, j: (bi, r, 0, 1)),
            pl.BlockSpec((1, 1, length, DIL_WIDTH), lambda bi, r, j: (bi, r, 0, 2)),
        ],
        out_specs=(out_spec, out_spec),
        compiler_params=_cparams(("parallel", "parallel", "arbitrary")),
        name=f"dilated_attention_g{group}",
    )(zg, zg, zg)


def _ffn_tail(x1, gf_ref, wg_ref, wu_ref, wd_ref, gfin_ref, o_ref, *, final):
    hn = _rmsnorm(x1, gf_ref[...]).astype(BF16)
    gate = _dot(hn, wg_ref[...])
    up = _dot(hn, wu_ref[...])
    act = gate * (1.0 / (1.0 + jnp.exp(-gate))) * up
    y = x1 + _dot(act.astype(BF16), wd_ref[...])
    if final:
        y = _rmsnorm(y, gfin_ref[...])
    o_ref[0] = y


def _token_major(ref, scr):
    dil, per_class = ref.shape[1], ref.shape[2]
    if dil == 1:
        return ref[0, 0].astype(F32)
    for r in range(dil):
        blk = ref[0, r].astype(F32)
        for c in range(DIL_WIDTH // LANES):
            scr[c, pl.ds(r, per_class, stride=dil), :] = blk[:, c * LANES:(c + 1) * LANES]
    return jnp.concatenate([scr[c] for c in range(DIL_WIDTH // LANES)], axis=-1)


def _even_out_kernel(x_ref, oa_ref, o0_ref, o1_ref, o2_ref, l0_ref, l1_ref, l2_ref,
                     woa_ref, wob_ref, gf_ref, wg_ref, wu_ref, wd_ref, gfin_ref, o_ref,
                     so1_scr, so2_scr, sl1_scr, sl2_scr, *, final):
    o0, o1, o2 = _token_major(o0_ref, None), _token_major(o1_ref, so1_scr), _token_major(o2_ref, so2_scr)
    l0, l1, l2 = _token_major(l0_ref, None), _token_major(l1_ref, sl1_scr), _token_major(l2_ref, sl2_scr)
    mx = jnp.maximum(jnp.maximum(l0, l1), l2)
    e0, e1, e2 = jnp.exp(l0 - mx), jnp.exp(l1 - mx), jnp.exp(l2 - mx)
    tot = e0 + e1 + e2
    comb = (e0 / tot) * o0 + (e1 / tot) * o1 + (e2 / tot) * o2
    x1 = x_ref[0] + (_dot(oa_ref[0], woa_ref[...]) + _dot(comb.astype(BF16), wob_ref[...]))
    _ffn_tail(x1, gf_ref, wg_ref, wu_ref, wd_ref, gfin_ref, o_ref, final=final)


def _odd_out_kernel(x_ref, o_in_ref, wo_ref, gf_ref, wg_ref, wu_ref, wd_ref, gfin_ref, o_ref,
                    *, final):
    x1 = x_ref[0] + _dot(o_in_ref[0], wo_ref[...])
    _ffn_tail(x1, gf_ref, wg_ref, wu_ref, wd_ref, gfin_ref, o_ref, final=final)


def _out_ffn(kernel_fn, x, acts, consts, *, ts, final, scratch=()):
    b, s, d = x.shape

    def row_spec(a):
        if a.ndim == 4:
            dil = a.shape[1]
            return pl.BlockSpec((1, dil, ts // dil, a.shape[-1]), lambda bi, si: (bi, 0, si, 0))
        return pl.BlockSpec((1, ts, a.shape[-1]), lambda bi, si: (bi, si, 0))

    return pl.pallas_call(
        functools.partial(kernel_fn, final=final),
        out_shape=jax.ShapeDtypeStruct((b, s, d), F32),
        grid=(b, s // ts),
        in_specs=[row_spec(x)] + [row_spec(a) for a in acts] + [_const_spec(c.shape) for c in consts],
        out_specs=row_spec(x),
        scratch_shapes=list(scratch),
        compiler_params=_cparams(("parallel", "parallel")),
        name=kernel_fn.__name__.strip("_"),
    )(x, *acts, *consts)


def _rope_angles(pos, dim):
    freqs = ROPE_THETA ** (-jnp.arange(0, dim, 2, dtype=F32) / dim)
    ang = pos.astype(F32)[:, None] * freqs[None, :]
    return jnp.cos(ang), jnp.sin(ang)


def _mla_tables(s):
    cos, sin = _rope_angles(jnp.arange(s), MLA_ROPE)
    one = jnp.ones((s, MLA_NOPE), F32)
    tail = LANES - MLA_NOPE - MLA_ROPE
    cos_t = jnp.concatenate([one, cos, cos, jnp.ones((s, tail), F32)], axis=-1)
    sin_t = jnp.concatenate([jnp.zeros((s, MLA_NOPE), F32), -sin, sin, jnp.zeros((s, tail), F32)], axis=-1)
    return cos_t, sin_t


def _axial_tables(s):
    rows = s // GRID_W
    row_idx = jnp.broadcast_to(jnp.arange(rows)[:, None], (rows, GRID_W)).reshape(-1)
    col_idx = jnp.broadcast_to(jnp.arange(GRID_W)[None, :], (rows, GRID_W)).reshape(-1)
    cos_r, sin_r = _rope_angles(row_idx, HEAD_DIM // 2)
    cos_c, sin_c = _rope_angles(col_idx, HEAD_DIM // 2)
    pad = LANES - HEAD_DIM
    cos_t = jnp.concatenate([cos_r, cos_r, cos_c, cos_c, jnp.ones((s, pad), F32)], axis=-1)
    sin_t = jnp.concatenate([-sin_r, sin_r, -sin_c, sin_c, jnp.zeros((s, pad), F32)], axis=-1)
    return cos_t, sin_t


def _head_slabs(w, fill):
    kdim, heads, d = w.shape
    return jnp.pad(w, ((0, 0), (0, 0), (0, fill - d))).reshape(kdim, heads * fill).astype(BF16)


def _row(v, width=None):
    v = v.reshape(1, -1).astype(F32)
    if width is not None:
        v = jnp.pad(v, ((0, 0), (0, width - v.shape[1])))
    return v


class _Tiles(NamedTuple):
    rows: int
    attn_rows: int
    key_chunk: int
    key_chunk_online: int
    dil_queries: int


def _tiles(s):
    return _Tiles(rows=min(512, s), attn_rows=min(2048, s), key_chunk=min(1024, s // 2),
                  key_chunk_online=min(512, s), dil_queries=256)


def kernel(x, mix_norm_ab, w_in_ab, mla_q_norm, mla_kv_norm, mla_w_uq, mla_w_ukv, w_out_ab,
           mix_norm_c, gqa_w_q, gqa_w_kv, gqa_q_norm, gqa_k_norm, gqa_w_o,
           ffn_norm, ffn_w_in, ffn_w_out, final_norm):
    b, s, d = x.shape
    depth = ffn_norm.shape[0]
    hidden = ffn_w_out.shape[1]
    tiles = _tiles(s)
    ts = tiles.rows
    in_a = MLA_Q_RANK + MLA_KV_RANK
    mla_cos, mla_sin = _mla_tables(s)
    ax_cos, ax_sin = _axial_tables(s)
    gfin = _row(final_norm)

    for layer in range(depth):
        i = layer // 2
        final = layer == depth - 1
        gf = _row(ffn_norm[layer])
        wg = ffn_w_in[layer][:, :hidden].astype(BF16)
        wu = ffn_w_in[layer][:, hidden:].astype(BF16)
        wd = ffn_w_out[layer].astype(BF16)
        if layer % 2 == 0:
            w_in = w_in_ab[i]
            wa = jnp.concatenate([
                w_in[:, :in_a], jnp.zeros((d, MLA_NOPE), F32), w_in[:, in_a:in_a + MLA_ROPE],
                jnp.zeros((d, LANES - MLA_NOPE - MLA_ROPE), F32)], axis=-1).astype(BF16)
            wb = w_in[:, in_a + MLA_ROPE:].astype(BF16)
            wuq = _head_slabs(mla_w_uq[i], LANES)
            wuk = _head_slabs(mla_w_ukv[i][:, :, :MLA_NOPE], LANES)
            wuv = _head_slabs(mla_w_ukv[i][:, :, MLA_NOPE:], LANES)
            q, k, v, *zgs = _even_in_proj(
                x, _row(mix_norm_ab[i]), wa, wb, _row(mla_q_norm[i]), _row(mla_kv_norm[i]),
                wuq, wuk, wuv, mla_cos, mla_sin, ts=ts)
            o_a = _attention(q, k, v, hb=2, tq=tiles.attn_rows, tk=tiles.key_chunk_online,
                             tk_fast=tiles.key_chunk, bias_lane=MLA_BIAS_LANE)
            outs, lses = zip(*[_dilated_group(zgs[g], g, qb_max=tiles.dil_queries)
                               for g in range(DIL_GROUPS)])
            n_a = MLA_HEADS * HEAD_DIM
            x = _out_ffn(_even_out_kernel, x, (o_a,) + outs + lses,
                         (w_out_ab[i][:n_a].astype(BF16), w_out_ab[i][n_a:].astype(BF16),
                          gf, wg, wu, wd, gfin), ts=ts, final=final,
                         scratch=[pltpu.VMEM((DIL_WIDTH // LANES, ts, LANES), F32)] * 4)
        else:
            wq = _head_slabs(gqa_w_q[i].reshape(d, GQA_HEADS, HEAD_DIM), LANES)
            wkv = gqa_w_kv[i].reshape(d, 2, GQA_KV_HEADS, HEAD_DIM)
            wk = _head_slabs(wkv[:, 0], LANES)
            wv = _head_slabs(wkv[:, 1], LANES)
            q, k, v = _odd_in_proj(
                x, _row(mix_norm_c[i]), wq, wk, wv, _row(gqa_q_norm[i], LANES),
                _row(gqa_k_norm[i], LANES), ax_cos, ax_sin, ts=ts)
            rep = GQA_HEADS // GQA_KV_HEADS
            o = _attention(q, k, v, hb=1, tq=max(tiles.attn_rows // rep, min(128, s)),
                           tk=tiles.key_chunk_online, tk_fast=tiles.key_chunk, bias_lane=GQA_BIAS_LANE)
            x = _out_ffn(_odd_out_kernel, x, (o,),
                         (gqa_w_o[i].astype(BF16), gf, wg, wu, wd, gfin), ts=ts, final=final)
    return x
```

```python
import functools
import math
from typing import NamedTuple

import jax
import jax.numpy as jnp
from jax import lax
from jax.experimental import pallas as pl
from jax.experimental.pallas import tpu as pltpu

F32 = jnp.float32
BF16 = jnp.bfloat16

NORM_EPS = 1e-6
ROPE_THETA = 10000.0
GRID_W = 64
LOG2E = math.log2(math.e)
LN2 = math.log(2.0)
MASKED = -1e30

LANES = 128
SUBLANES = 8
ROPE_HALF = 16
TINY = 1e-30
HEAD_DIM = 64

MLA_HEADS = 8
MLA_Q_RANK = 384
MLA_KV_RANK = 256
MLA_NOPE = 64
MLA_ROPE = 32

DIL_PAIRS = ((128, 1), (512, 4), (2048, 16))
DIL_HALF = 64
DIL_SLOTS = 4
DIL_GROUPS = 3
DIL_HEADS = DIL_SLOTS * DIL_GROUPS
DIL_WIDTH = DIL_SLOTS * HEAD_DIM

GQA_HEADS = 16
GQA_KV_HEADS = 4

VT_ROWS = 128
MLA_BIAS_LANE = MLA_NOPE + MLA_ROPE
GQA_BIAS_LANE = HEAD_DIM

FAST_BOUND_LIMIT = 60.0
BOUND_SLACK = 1.02

VMEM_LIMIT = 56 * 1024 * 1024


def _cparams(sem):
    return pltpu.CompilerParams(dimension_semantics=sem, vmem_limit_bytes=VMEM_LIMIT)


def _const_spec(shape):
    nd = len(shape)
    return pl.BlockSpec(shape, lambda *_: (0,) * nd, pipeline_mode=pl.Buffered(1))


def _rmsnorm(x, g):
    ms = jnp.mean(x * x, axis=-1, keepdims=True)
    return x * lax.rsqrt(ms + NORM_EPS) * g


def _dot(a, b):
    return jnp.dot(a, b, preferred_element_type=F32)


def _rope_slab(t, cos_t, sin_t, first_half):
    partner = jnp.where(first_half, pltpu.roll(t, LANES - ROPE_HALF, 1), pltpu.roll(t, ROPE_HALF, 1))
    return t * cos_t + partner * sin_t


def _first_half_mask(rows):
    lane = lax.broadcasted_iota(jnp.int32, (rows, LANES), 1)
    return (lane % (2 * ROPE_HALF)) < ROPE_HALF


def _even_in_kernel(x_ref, g_ref, wa_ref, wb_ref, gq_ref, gkv_ref, wuq_ref, wuk_ref, wuv_ref,
                    cos_ref, sin_ref, q_ref, k_ref, v_ref, z0_ref, z1_ref, z2_ref, zs_ref,
                    *, q_scale, dil_scale):
    ts = x_ref.shape[1]
    xn = _rmsnorm(x_ref[0], g_ref[...]).astype(BF16)
    za = _dot(xn, wa_ref[...])
    zb = _dot(xn, wb_ref[...])
    nq = DIL_GROUPS * DIL_WIDTH
    for g, zg_ref in enumerate((z0_ref, z1_ref, z2_ref)):
        dil = DIL_PAIRS[g][1]
        lo = g * DIL_WIDTH
        parts = (zb[:, lo:lo + DIL_WIDTH] * dil_scale,
                 zb[:, nq + lo:nq + lo + DIL_WIDTH],
                 zb[:, 2 * nq + lo:2 * nq + lo + DIL_WIDTH])
        if dil == 1:
            for c, part in enumerate(parts):
                zg_ref[0, 0, :, c * DIL_WIDTH:(c + 1) * DIL_WIDTH] = part.astype(BF16)
            continue
        n_slabs = 3 * DIL_WIDTH // LANES
        for c in range(n_slabs):
            part = parts[c * LANES // DIL_WIDTH]
            off = c * LANES % DIL_WIDTH
            zs_ref[c] = part[:, off:off + LANES]
        for r in range(dil):
            for c in range(n_slabs):
                rows = zs_ref[c, pl.ds(r, ts // dil, stride=dil), :]
                zg_ref[0, r, :, c * LANES:(c + 1) * LANES] = rows.astype(BF16)

    cq = _rmsnorm(za[:, :MLA_Q_RANK], gq_ref[...]).astype(BF16)
    ckv = _rmsnorm(za[:, MLA_Q_RANK:MLA_Q_RANK + MLA_KV_RANK], gkv_ref[...]).astype(BF16)
    q_all = _dot(cq, wuq_ref[...])
    k_all = _dot(ckv, wuk_ref[...])
    v_all = _dot(ckv, wuv_ref[...])

    cos_t = cos_ref[...]
    sin_t = sin_ref[...]
    first = _first_half_mask(ts)
    lane = lax.broadcasted_iota(jnp.int32, (ts, LANES), 1)
    ones_col = (lane == HEAD_DIM).astype(F32)
    k_rope = _rope_slab(za[:, MLA_Q_RANK + MLA_KV_RANK:], cos_t, sin_t, first) \
        + (lane == MLA_BIAS_LANE).astype(F32)
    for h in range(MLA_HEADS):
        sl = slice(h * LANES, (h + 1) * LANES)
        q_ref[0, h] = _rope_slab(q_all[:, sl] * q_scale, cos_t, sin_t, first).astype(BF16)
        k_ref[0, h] = (k_all[:, sl] + k_rope).astype(BF16)
        v_ref[0, h] = (v_all[:, sl] + ones_col).T[:VT_ROWS].astype(BF16)


def _even_in_proj(x, g, wa, wb, gq, gkv, wuq, wuk, wuv, cos_t, sin_t, *, ts):
    b, s, d = x.shape
    n_s = s // ts
    head_out = jax.ShapeDtypeStruct((b, MLA_HEADS, s, LANES), BF16)
    head_spec = pl.BlockSpec((1, MLA_HEADS, ts, LANES), lambda bi, si: (bi, 0, si, 0))
    q_scale = (MLA_NOPE + MLA_ROPE) ** -0.5 * LOG2E
    dil_scale = HEAD_DIM ** -0.5 * LOG2E
    zg_outs, zg_specs = [], []
    for _, dil in DIL_PAIRS:
        zg_outs.append(jax.ShapeDtypeStruct((b, dil, s // dil, 3 * DIL_WIDTH), BF16))
        zg_specs.append(pl.BlockSpec((1, dil, ts // dil, 3 * DIL_WIDTH), lambda bi, si: (bi, 0, si, 0)))
    return pl.pallas_call(
        functools.partial(_even_in_kernel, q_scale=q_scale, dil_scale=dil_scale),
        out_shape=(head_out, head_out, jax.ShapeDtypeStruct((b, MLA_HEADS, VT_ROWS, s), BF16),
                   *zg_outs),
        grid=(b, n_s),
        in_specs=[
            pl.BlockSpec((1, ts, d), lambda bi, si: (bi, si, 0)),
            _const_spec(g.shape), _const_spec(wa.shape), _const_spec(wb.shape),
            _const_spec(gq.shape), _const_spec(gkv.shape),
            _const_spec(wuq.shape), _const_spec(wuk.shape), _const_spec(wuv.shape),
            pl.BlockSpec((ts, LANES), lambda bi, si: (si, 0)),
            pl.BlockSpec((ts, LANES), lambda bi, si: (si, 0)),
        ],
        out_specs=(head_spec, head_spec,
                   pl.BlockSpec((1, MLA_HEADS, VT_ROWS, ts), lambda bi, si: (bi, 0, 0, si)),
                   *zg_specs),
        scratch_shapes=[pltpu.VMEM((3 * DIL_WIDTH // LANES, ts, LANES), F32)],
        compiler_params=_cparams(("parallel", "parallel")),
        name="even_in_proj",
    )(x, g, wa, wb, gq, gkv, wuq, wuk, wuv, cos_t, sin_t)


def _odd_in_kernel(x_ref, g_ref, wq_ref, wk_ref, wv_ref, gq_ref, gk_ref, cos_ref, sin_ref,
                   q_ref, k_ref, v_ref, *, q_scale):
    ts = x_ref.shape[1]
    xn = _rmsnorm(x_ref[0], g_ref[...]).astype(BF16)
    q_all = _dot(xn, wq_ref[...])
    k_all = _dot(xn, wk_ref[...])
    v_all = _dot(xn, wv_ref[...])
    cos_t = cos_ref[...]
    sin_t = sin_ref[...]
    first = _first_half_mask(ts)
    lane = lax.broadcasted_iota(jnp.int32, (ts, LANES), 1)
    ones_col = (lane == HEAD_DIM).astype(F32)

    def head_norm(t, gain):
        ms = jnp.sum(t * t, axis=-1, keepdims=True) * (1.0 / HEAD_DIM)
        return t * lax.rsqrt(ms + NORM_EPS) * gain

    gq = gq_ref[...] * q_scale
    gk = gk_ref[...]
    for h in range(GQA_HEADS):
        sl = slice(h * LANES, (h + 1) * LANES)
        q_ref[0, h] = _rope_slab(head_norm(q_all[:, sl], gq), cos_t, sin_t, first).astype(BF16)
    for h in range(GQA_KV_HEADS):
        sl = slice(h * LANES, (h + 1) * LANES)
        k_ref[0, h] = (_rope_slab(head_norm(k_all[:, sl], gk), cos_t, sin_t, first) + ones_col).astype(BF16)
        v_ref[0, h] = (v_all[:, sl] + ones_col).T[:VT_ROWS].astype(BF16)


def _odd_in_proj(x, g, wq, wk, wv, gq, gk, cos_t, sin_t, *, ts):
    b, s, d = x.shape
    n_s = s // ts
    q_scale = HEAD_DIM ** -0.5 * LOG2E
    return pl.pallas_call(
        functools.partial(_odd_in_kernel, q_scale=q_scale),
        out_shape=(jax.ShapeDtypeStruct((b, GQA_HEADS, s, LANES), BF16),
                   jax.ShapeDtypeStruct((b, GQA_KV_HEADS, s, LANES), BF16),
                   jax.ShapeDtypeStruct((b, GQA_KV_HEADS, VT_ROWS, s), BF16)),
        grid=(b, n_s),
        in_specs=[
            pl.BlockSpec((1, ts, d), lambda bi, si: (bi, si, 0)),
            _const_spec(g.shape), _const_spec(wq.shape), _const_spec(wk.shape),
            _const_spec(wv.shape), _const_spec(gq.shape), _const_spec(gk.shape),
            pl.BlockSpec((ts, LANES), lambda bi, si: (si, 0)),
            pl.BlockSpec((ts, LANES), lambda bi, si: (si, 0)),
        ],
        out_specs=(pl.BlockSpec((1, GQA_HEADS, ts, LANES), lambda bi, si: (bi, 0, si, 0)),
                   pl.BlockSpec((1, GQA_KV_HEADS, ts, LANES), lambda bi, si: (bi, 0, si, 0)),
                   pl.BlockSpec((1, GQA_KV_HEADS, VT_ROWS, ts), lambda bi, si: (bi, 0, 0, si))),
        compiler_params=_cparams(("parallel", "parallel")),
        name="odd_in_proj",
    )(x, g, wq, wk, wv, gq, gk, cos_t, sin_t)


_NT_DIMS = (((1,), (1,)), ((), ()))


def _attn_kernel(q_ref, k_ref, v_ref, o_ref, kmax_scr, m_scr, acc_scr, p_scr,
                 *, hb, rep, tk, tk_fast, bias_lane, fast_limit):
    tq = q_ref.shape[2]
    s_len = k_ref.shape[2]
    m_rows = rep * tq

    @pl.when(pl.program_id(2) == 0)
    def _():
        ones = jnp.ones((LANES, LANES), BF16)
        for h in range(hb):
            def kbody(j, best, h=h):
                start = pl.multiple_of(j * tk, tk)
                kc = k_ref[0, h, pl.ds(start, tk), :].astype(F32)
                kn2 = _dot((kc * kc).astype(BF16), ones)
                return jnp.maximum(best, jnp.max(kn2, axis=0, keepdims=True))

            best = lax.fori_loop(0, s_len // tk, kbody, jnp.zeros((1, LANES), F32))
            kmax_scr[h] = jnp.broadcast_to(best, (SUBLANES, LANES))

    lane = lax.broadcasted_iota(jnp.int32, (m_rows, LANES), 1)
    outs = []
    for h in range(hb):
        q2 = q_ref[0, h * rep:(h + 1) * rep].reshape(m_rows, LANES)
        qf = q2.astype(F32)
        qn2 = _dot((qf * qf).astype(BF16), jnp.ones((LANES, LANES), BF16))
        bound = qn2 * lax.rsqrt(qn2 + TINY) * (jnp.sqrt(kmax_scr[h][0:1, :]) * BOUND_SLACK)
        use_fast = jnp.max(bound) <= fast_limit
        acc_scr[...] = jnp.zeros((VT_ROWS, m_rows), F32)

        @pl.when(use_fast)
        def _(h=h, qf=qf, bound=bound):
            q_aug = jnp.where(lane == bias_lane, -bound, qf).astype(BF16)

            def probs(j):
                start = pl.multiple_of(j * tk_fast, tk_fast)
                kc = k_ref[0, h, pl.ds(start, tk_fast), :]
                st = lax.dot_general(kc, q_aug, _NT_DIMS, preferred_element_type=F32)
                return jnp.exp2(st).astype(BF16)

            def accumulate(j, slot):
                start = pl.multiple_of(j * tk_fast, tk_fast)
                acc_scr[...] += _dot(v_ref[0, h, :, pl.ds(start, tk_fast)], p_scr[slot])

            n_chunks = s_len // tk_fast
            p_scr[0] = probs(0)

            def body(i, carry):
                nxt = probs(2 * i + 1)
                accumulate(2 * i, 0)
                p_scr[1] = nxt
                nxt = probs(2 * i + 2)
                accumulate(2 * i + 1, 1)
                p_scr[0] = nxt
                return carry

            lax.fori_loop(0, n_chunks // 2 - 1, body, 0)
            nxt = probs(n_chunks - 1)
            accumulate(n_chunks - 2, 0)
            p_scr[1] = nxt
            accumulate(n_chunks - 1, 1)

        @pl.when(jnp.logical_not(use_fast))
        def _(h=h, q2=q2):
            m_scr[...] = jnp.full((1, m_rows), MASKED, F32)

            def body(j, carry):
                start = pl.multiple_of(j * tk, tk)
                kc = k_ref[0, h, pl.ds(start, tk), :]
                vtc = v_ref[0, h, :, pl.ds(start, tk)]
                st = lax.dot_general(kc, q2, _NT_DIMS, preferred_element_type=F32)
                m_old = m_scr[...]
                m_new = jnp.maximum(m_old, jnp.max(st, axis=0, keepdims=True))
                p = jnp.exp2(st - m_new)
                alpha = jnp.exp2(m_old - m_new)
                acc_scr[...] = alpha * acc_scr[...] + _dot(vtc, p.astype(BF16))
                m_scr[...] = m_new
                return carry

            lax.fori_loop(0, s_len // tk, body, 0)

        acc = acc_scr[...]
        o = (acc[:HEAD_DIM] / acc[HEAD_DIM:HEAD_DIM + 1]).T
        for r in range(rep):
            outs.append(o[r * tq:(r + 1) * tq])
    o_ref[0] = jnp.concatenate(outs, axis=-1).astype(o_ref.dtype)


def _attention(q, k, v, *, hb, tq, tk, tk_fast, bias_lane, fast_limit=FAST_BOUND_LIMIT):
    b, hq, s, _ = q.shape
    hk = k.shape[1]
    rep = hq // hk
    width = hb * rep * HEAD_DIM
    m_rows = rep * tq
    assert (s // tk_fast) % 2 == 0, "the two-slot pipeline consumes key chunks in pairs"
    return pl.pallas_call(
        functools.partial(_attn_kernel, hb=hb, rep=rep, tk=tk, tk_fast=tk_fast,
                          bias_lane=bias_lane, fast_limit=fast_limit),
        out_shape=jax.ShapeDtypeStruct((b, s, hq * HEAD_DIM), BF16),
        grid=(b, hk // hb, s // tq),
        in_specs=[
            pl.BlockSpec((1, hb * rep, tq, LANES), lambda bi, gi, qi: (bi, gi, qi, 0)),
            pl.BlockSpec((1, hb, s, LANES), lambda bi, gi, qi: (bi, gi, 0, 0)),
            pl.BlockSpec((1, hb, VT_ROWS, s), lambda bi, gi, qi: (bi, gi, 0, 0)),
        ],
        out_specs=pl.BlockSpec((1, tq, width), lambda bi, gi, qi: (bi, qi, gi)),
        scratch_shapes=[pltpu.VMEM((hb, SUBLANES, LANES), F32), pltpu.VMEM((1, m_rows), F32),
                        pltpu.VMEM((VT_ROWS, m_rows), F32), pltpu.VMEM((2, tk_fast, m_rows), BF16)],
        compiler_params=_cparams(("parallel", "parallel", "arbitrary")),
        name="dense_attention",
    )(q, k, v)


def _dilated_kernel(q_ref, k_ref, v_ref, o_ref, lse_ref, *, dilation, slopes, qb, win):
    length = k_ref.shape[2]
    u0 = pl.program_id(2) * qb
    start = jnp.clip(u0 - DIL_HALF, 0, length - win)
    start = pl.multiple_of(start, DIL_HALF)
    kw = k_ref[0, 0, pl.ds(start, win), :]
    vw = v_ref[0, 0, pl.ds(start, win), :]
    q = q_ref[0, 0]
    row = lax.broadcasted_iota(jnp.int32, (qb, win), 0)
    col = lax.broadcasted_iota(jnp.int32, (qb, win), 1)
    rel = col - row + (start - u0)
    dist = jnp.abs(rel)
    dist_m = jnp.where(dist <= DIL_HALF, dist.astype(F32), -MASKED)
    lane = lax.broadcasted_iota(jnp.int32, (qb, DIL_WIDTH), 1)
    o_all = jnp.zeros((qb, DIL_WIDTH), F32)
    lse_all = jnp.zeros((qb, DIL_WIDTH), F32)
    for sl in range(DIL_SLOTS):
        in_slot = (lane >= sl * HEAD_DIM) & (lane < (sl + 1) * HEAD_DIM)
        q_s = jnp.where(in_slot, q, jnp.zeros_like(q))
        s = lax.dot_general(q_s, kw, _NT_DIMS, preferred_element_type=F32)
        s = s - (slopes[sl] * dilation * LOG2E) * dist_m
        m = jnp.max(s, axis=-1, keepdims=True)
        e = jnp.exp2(s - m)
        den = jnp.sum(e, axis=-1, keepdims=True)
        o = _dot(e.astype(BF16), vw) / den
        lse = (m + jnp.log2(den)) * LN2
        o_all = jnp.where(in_slot, o, o_all)
        lse_all = jnp.where(in_slot, lse, lse_all)
    o_ref[0, 0] = o_all.astype(o_ref.dtype)
    lse_ref[0, 0] = lse_all


def _dilated_group(zg, group, *, qb_max):
    b, dilation, length, _ = zg.shape
    assert dilation == DIL_PAIRS[group][1]
    qb = min(qb_max, length)
    win = min(length, qb + 2 * DIL_HALF)
    slopes = tuple(2.0 ** (-8.0 * (group * DIL_SLOTS + i + 1) / DIL_HEADS) for i in range(DIL_SLOTS))
    out_spec = pl.BlockSpec((1, 1, qb, DIL_WIDTH), lambda bi, r, j: (bi, r, j, 0))
    return pl.pallas_call(
        functools.partial(_dilated_kernel, dilation=dilation, slopes=slopes, qb=qb, win=win),
        out_shape=(jax.ShapeDtypeStruct((b, dilation, length, DIL_WIDTH), BF16),
                   jax.ShapeDtypeStruct((b, dilation, length, DIL_WIDTH), F32)),
        grid=(b, dilation, length // qb),
        in_specs=[
            pl.BlockSpec((1, 1, qb, DIL_WIDTH), lambda bi, r, j: (bi, r, j, 0)),
            pl.BlockSpec((1, 1, length, DIL_WIDTH), lambda bi, r, j: (bi, r, 0, 1)),
            pl.BlockSpec((1, 1, length, DIL_WIDTH), lambda bi, r, j: (bi, r, 0, 2)),
        ],
        out_specs=(out_spec, out_spec),
        compiler_params=_cparams(("parallel", "parallel", "arbitrary")),
        name=f"dilated_attention_g{group}",
    )(zg, zg, zg)


def _ffn_tail(x1, gf_ref, wg_ref, wu_ref, wd_ref, gfin_ref, o_ref, *, final):
    hn = _rmsnorm(x1, gf_ref[...]).astype(BF16)
    gate = _dot(hn, wg_ref[...])
    up = _dot(hn, wu_ref[...])
    act = gate * (1.0 / (1.0 + jnp.exp(-gate))) * up
    y = x1 + _dot(act.astype(BF16), wd_ref[...])
    if final:
        y = _rmsnorm(y, gfin_ref[...])
    o_ref[0] = y


def _token_major(ref, scr):
    dil, per_class = ref.shape[1], ref.shape[2]
    if dil == 1:
        return ref[0, 0].astype(F32)
    for r in range(dil):
        blk = ref[0, r].astype(F32)
        for c in range(DIL_WIDTH // LANES):
            scr[c, pl.ds(r, per_class, stride=dil), :] = blk[:, c * LANES:(c + 1) * LANES]
    return jnp.concatenate([scr[c] for c in range(DIL_WIDTH // LANES)], axis=-1)


def _even_out_kernel(x_ref, oa_ref, o0_ref, o1_ref, o2_ref, l0_ref, l1_ref, l2_ref,
                     woa_ref, wob_ref, gf_ref, wg_ref, wu_ref, wd_ref, gfin_ref, o_ref,
                     so1_scr, so2_scr, sl1_scr, sl2_scr, *, final):
    o0, o1, o2 = _token_major(o0_ref, None), _token_major(o1_ref, so1_scr), _token_major(o2_ref, so2_scr)
    l0, l1, l2 = _token_major(l0_ref, None), _token_major(l1_ref, sl1_scr), _token_major(l2_ref, sl2_scr)
    mx = jnp.maximum(jnp.maximum(l0, l1), l2)
    e0, e1, e2 = jnp.exp(l0 - mx), jnp.exp(l1 - mx), jnp.exp(l2 - mx)
    tot = e0 + e1 + e2
    comb = (e0 / tot) * o0 + (e1 / tot) * o1 + (e2 / tot) * o2
    x1 = x_ref[0] + (_dot(oa_ref[0], woa_ref[...]) + _dot(comb.astype(BF16), wob_ref[...]))
    _ffn_tail(x1, gf_ref, wg_ref, wu_ref, wd_ref, gfin_ref, o_ref, final=final)


def _odd_out_kernel(x_ref, o_in_ref, wo_ref, gf_ref, wg_ref, wu_ref, wd_ref, gfin_ref, o_ref,
                    *, final):
    x1 = x_ref[0] + _dot(o_in_ref[0], wo_ref[...])
    _ffn_tail(x1, gf_ref, wg_ref, wu_ref, wd_ref, gfin_ref, o_ref, final=final)


def _out_ffn(kernel_fn, x, acts, consts, *, ts, final, scratch=()):
    b, s, d = x.shape

    def row_spec(a):
        if a.ndim == 4:
            dil = a.shape[1]
            return pl.BlockSpec((1, dil, ts // dil, a.shape[-1]), lambda bi, si: (bi, 0, si, 0))
        return pl.BlockSpec((1, ts, a.shape[-1]), lambda bi, si: (bi, si, 0))

    return pl.pallas_call(
        functools.partial(kernel_fn, final=final),
        out_shape=jax.ShapeDtypeStruct((b, s, d), F32),
        grid=(b, s // ts),
        in_specs=[row_spec(x)] + [row_spec(a) for a in acts] + [_const_spec(c.shape) for c in consts],
        out_specs=row_spec(x),
        scratch_shapes=list(scratch),
        compiler_params=_cparams(("parallel", "parallel")),
        name=kernel_fn.__name__.strip("_"),
    )(x, *acts, *consts)


def _rope_angles(pos, dim):
    freqs = ROPE_THETA ** (-jnp.arange(0, dim, 2, dtype=F32) / dim)
    ang = pos.astype(F32)[:, None] * freqs[None, :]
    return jnp.cos(ang), jnp.sin(ang)


def _mla_tables(s):
    cos, sin = _rope_angles(jnp.arange(s), MLA_ROPE)
    one = jnp.ones((s, MLA_NOPE), F32)
    tail = LANES - MLA_NOPE - MLA_ROPE
    cos_t = jnp.concatenate([one, cos, cos, jnp.ones((s, tail), F32)], axis=-1)
    sin_t = jnp.concatenate([jnp.zeros((s, MLA_NOPE), F32), -sin, sin, jnp.zeros((s, tail), F32)], axis=-1)
    return cos_t, sin_t


def _axial_tables(s):
    rows = s // GRID_W
    row_idx = jnp.broadcast_to(jnp.arange(rows)[:, None], (rows, GRID_W)).reshape(-1)
    col_idx = jnp.broadcast_to(jnp.arange(GRID_W)[None, :], (rows, GRID_W)).reshape(-1)
    cos_r, sin_r = _rope_angles(row_idx, HEAD_DIM // 2)
    cos_c, sin_c = _rope_angles(col_idx, HEAD_DIM // 2)
    pad = LANES - HEAD_DIM
    cos_t = jnp.concatenate([cos_r, cos_r, cos_c, cos_c, jnp.ones((s, pad), F32)], axis=-1)
    sin_t = jnp.concatenate([-sin_r, sin_r, -sin_c, sin_c, jnp.zeros((s, pad), F32)], axis=-1)
    return cos_t, sin_t


def _head_slabs(w, fill):
    kdim, heads, d = w.shape
    return jnp.pad(w, ((0, 0), (0, 0), (0, fill - d))).reshape(kdim, heads * fill).astype(BF16)


def _row(v, width=None):
    v = v.reshape(1, -1).astype(F32)
    if width is not None:
        v = jnp.pad(v, ((0, 0), (0, width - v.shape[1])))
    return v


class _Tiles(NamedTuple):
    rows: int
    attn_rows: int
    key_chunk: int
    key_chunk_online: int
    dil_queries: int


def _tiles(s):
    return _Tiles(rows=min(512, s), attn_rows=min(2048, s), key_chunk=min(1024, s // 2),
                  key_chunk_online=min(512, s), dil_queries=256)


def kernel(x, mix_norm_ab, w_in_ab, mla_q_norm, mla_kv_norm, mla_w_uq, mla_w_ukv, w_out_ab,
           mix_norm_c, gqa_w_q, gqa_w_kv, gqa_q_norm, gqa_k_norm, gqa_w_o,
           ffn_norm, ffn_w_in, ffn_w_out, final_norm):
    b, s, d = x.shape
    depth = ffn_norm.shape[0]
    hidden = ffn_w_out.shape[1]
    tiles = _tiles(s)
    ts = tiles.rows
    in_a = MLA_Q_RANK + MLA_KV_RANK
    mla_cos, mla_sin = _mla_tables(s)
    ax_cos, ax_sin = _axial_tables(s)
    gfin = _row(final_norm)

    for layer in range(depth):
        i = layer // 2
        final = layer == depth - 1
        gf = _row(ffn_norm[layer])
        wg = ffn_w_in[layer][:, :hidden].astype(BF16)
        wu = ffn_w_in[layer][:, hidden:].astype(BF16)
        wd = ffn_w_out[layer].astype(BF16)
        if layer % 2 == 0:
            w_in = w_in_ab[i]
            wa = jnp.concatenate([
                w_in[:, :in_a], jnp.zeros((d, MLA_NOPE), F32), w_in[:, in_a:in_a + MLA_ROPE],
                jnp.zeros((d, LANES - MLA_NOPE - MLA_ROPE), F32)], axis=-1).astype(BF16)
            wb = w_in[:, in_a + MLA_ROPE:].astype(BF16)
            wuq = _head_slabs(mla_w_uq[i], LANES)
            wuk = _head_slabs(mla_w_ukv[i][:, :, :MLA_NOPE], LANES)
            wuv = _head_slabs(mla_w_ukv[i][:, :, MLA_NOPE:], LANES)
            q, k, v, *zgs = _even_in_proj(
                x, _row(mix_norm_ab[i]), wa, wb, _row(mla_q_norm[i]), _row(mla_kv_norm[i]),
                wuq, wuk, wuv, mla_cos, mla_sin, ts=ts)
            o_a = _attention(q, k, v, hb=2, tq=tiles.attn_rows, tk=tiles.key_chunk_online,
                             tk_fast=tiles.key_chunk, bias_lane=MLA_BIAS_LANE)
            outs, lses = zip(*[_dilated_group(zgs[g], g, qb_max=tiles.dil_queries)
                               for g in range(DIL_GROUPS)])
            n_a = MLA_HEADS * HEAD_DIM
            x = _out_ffn(_even_out_kernel, x, (o_a,) + outs + lses,
                         (w_out_ab[i][:n_a].astype(BF16), w_out_ab[i][n_a:].astype(BF16),
                          gf, wg, wu, wd, gfin), ts=ts, final=final,
                         scratch=[pltpu.VMEM((DIL_WIDTH // LANES, ts, LANES), F32)] * 4)
        else:
            wq = _head_slabs(gqa_w_q[i].reshape(d, GQA_HEADS, HEAD_DIM), LANES)
            wkv = gqa_w_kv[i].reshape(d, 2, GQA_KV_HEADS, HEAD_DIM)
            wk = _head_slabs(wkv[:, 0], LANES)
            wv = _head_slabs(wkv[:, 1], LANES)
            q, k, v = _odd_in_proj(
                x, _row(mix_norm_c[i]), wq, wk, wv, _row(gqa_q_norm[i], LANES),
                _row(gqa_k_norm[i], LANES), ax_cos, ax_sin, ts=ts)
            rep = GQA_HEADS // GQA_KV_HEADS
            o = _attention(q, k, v, hb=1, tq=max(tiles.attn_rows // rep, min(128, s)),
                           tk=tiles.key_chunk_online, tk_fast=tiles.key_chunk, bias_lane=GQA_BIAS_LANE)
            x = _out_ffn(_odd_out_kernel, x, (o,),
                         (gqa_w_o[i].astype(BF16), gf, wg, wu, wd, gfin), ts=ts, final=final)
    return x
```

```python
import functools
import math
from typing import NamedTuple

import jax
import jax.numpy as jnp
from jax import lax
from jax.experimental import pallas as pl
from jax.experimental.pallas import tpu as pltpu

F32 = jnp.float32
BF16 = jnp.bfloat16

NORM_EPS = 1e-6
ROPE_THETA = 10000.0
GRID_W = 64
LOG2E = math.log2(math.e)
LN2 = math.log(2.0)
MASKED = -1e30

LANES = 128
SUBLANES = 8
ROPE_HALF = 16
HEAD_DIM = 64

MLA_HEADS = 8
MLA_Q_RANK = 384
MLA_KV_RANK = 256
MLA_NOPE = 64
MLA_ROPE = 32

DIL_PAIRS = ((128, 1), (512, 4), (2048, 16))
DIL_HALF = 64
DIL_SLOTS = 4
DIL_GROUPS = 3
DIL_HEADS = DIL_SLOTS * DIL_GROUPS
DIL_WIDTH = DIL_SLOTS * HEAD_DIM

GQA_HEADS = 16
GQA_KV_HEADS = 4

VT_ROWS = 128
MLA_BIAS_LANE = MLA_NOPE + MLA_ROPE
GQA_BIAS_LANE = HEAD_DIM

FAST_BOUND_LIMIT = 60.0
BOUND_SLACK = 1.02

VMEM_LIMIT = 56 * 1024 * 1024


def _cparams(sem):
    return pltpu.CompilerParams(dimension_semantics=sem, vmem_limit_bytes=VMEM_LIMIT)


def _const_spec(shape):
    nd = len(shape)
    return pl.BlockSpec(shape, lambda *_: (0,) * nd, pipeline_mode=pl.Buffered(1))


def _rmsnorm(x, g):
    ms = jnp.mean(x * x, axis=-1, keepdims=True)
    return x * lax.rsqrt(ms + NORM_EPS) * g


def _dot(a, b):
    return jnp.dot(a, b, preferred_element_type=F32)


def _rope_slab(t, cos_t, sin_t, first_half):
    partner = jnp.where(first_half, pltpu.roll(t, LANES - ROPE_HALF, 1), pltpu.roll(t, ROPE_HALF, 1))
    return t * cos_t + partner * sin_t


def _first_half_mask(rows):
    lane = lax.broadcasted_iota(jnp.int32, (rows, LANES), 1)
    return (lane % (2 * ROPE_HALF)) < ROPE_HALF


def _even_in_kernel(x_ref, g_ref, wa_ref, wb_ref, gq_ref, gkv_ref, wuq_ref, wuk_ref, wuv_ref,
                    cos_ref, sin_ref, q_ref, k_ref, v_ref, z0_ref, z1_ref, z2_ref, zs_ref,
                    *, q_scale, dil_scale):
    ts = x_ref.shape[1]
    xn = _rmsnorm(x_ref[0], g_ref[...]).astype(BF16)
    za = _dot(xn, wa_ref[...])
    zb = _dot(xn, wb_ref[...])
    nq = DIL_GROUPS * DIL_WIDTH
    for g, zg_ref in enumerate((z0_ref, z1_ref, z2_ref)):
        dil = DIL_PAIRS[g][1]
        lo = g * DIL_WIDTH
        parts = (zb[:, lo:lo + DIL_WIDTH] * dil_scale,
                 zb[:, nq + lo:nq + lo + DIL_WIDTH],
                 zb[:, 2 * nq + lo:2 * nq + lo + DIL_WIDTH])
        if dil == 1:
            for c, part in enumerate(parts):
                zg_ref[0, 0, :, c * DIL_WIDTH:(c + 1) * DIL_WIDTH] = part.astype(BF16)
            continue
        n_slabs = 3 * DIL_WIDTH // LANES
        for c in range(n_slabs):
            part = parts[c * LANES // DIL_WIDTH]
            off = c * LANES % DIL_WIDTH
            zs_ref[c] = part[:, off:off + LANES]
        for r in range(dil):
            for c in range(n_slabs):
                rows = zs_ref[c, pl.ds(r, ts // dil, stride=dil), :]
                zg_ref[0, r, :, c * LANES:(c + 1) * LANES] = rows.astype(BF16)

    cq = _rmsnorm(za[:, :MLA_Q_RANK], gq_ref[...]).astype(BF16)
    ckv = _rmsnorm(za[:, MLA_Q_RANK:MLA_Q_RANK + MLA_KV_RANK], gkv_ref[...]).astype(BF16)
    q_all = _dot(cq, wuq_ref[...])
    k_all = _dot(ckv, wuk_ref[...])
    v_all = _dot(ckv, wuv_ref[...])

    cos_t = cos_ref[...]
    sin_t = sin_ref[...]
    first = _first_half_mask(ts)
    lane = lax.broadcasted_iota(jnp.int32, (ts, LANES), 1)
    ones_col = (lane == HEAD_DIM).astype(F32)
    k_rope = _rope_slab(za[:, MLA_Q_RANK + MLA_KV_RANK:], cos_t, sin_t, first) \
        + (lane == MLA_BIAS_LANE).astype(F32)
    for h in range(MLA_HEADS):
        sl = slice(h * LANES, (h + 1) * LANES)
        q_ref[0, h] = _rope_slab(q_all[:, sl] * q_scale, cos_t, sin_t, first).astype(BF16)
        k_ref[0, h] = (k_all[:, sl] + k_rope).astype(BF16)
        v_ref[0, h] = (v_all[:, sl] + ones_col).T[:VT_ROWS].astype(BF16)


def _even_in_proj(x, g, wa, wb, gq, gkv, wuq, wuk, wuv, cos_t, sin_t, *, ts):
    b, s, d = x.shape
    n_s = s // ts
    head_out = jax.ShapeDtypeStruct((b, MLA_HEADS, s, LANES), BF16)
    head_spec = pl.BlockSpec((1, MLA_HEADS, ts, LANES), lambda bi, si: (bi, 0, si, 0))
    q_scale = (MLA_NOPE + MLA_ROPE) ** -0.5 * LOG2E
    dil_scale = HEAD_DIM ** -0.5 * LOG2E
    zg_outs, zg_specs = [], []
    for _, dil in DIL_PAIRS:
        zg_outs.append(jax.ShapeDtypeStruct((b, dil, s // dil, 3 * DIL_WIDTH), BF16))
        zg_specs.append(pl.BlockSpec((1, dil, ts // dil, 3 * DIL_WIDTH), lambda bi, si: (bi, 0, si, 0)))
    return pl.pallas_call(
        functools.partial(_even_in_kernel, q_scale=q_scale, dil_scale=dil_scale),
        out_shape=(head_out, head_out, jax.ShapeDtypeStruct((b, MLA_HEADS, VT_ROWS, s), BF16),
                   *zg_outs),
        grid=(b, n_s),
        in_specs=[
            pl.BlockSpec((1, ts, d), lambda bi, si: (bi, si, 0)),
            _const_spec(g.shape), _const_spec(wa.shape), _const_spec(wb.shape),
            _const_spec(gq.shape), _const_spec(gkv.shape),
            _const_spec(wuq.shape), _const_spec(wuk.shape), _const_spec(wuv.shape),
            pl.BlockSpec((ts, LANES), lambda bi, si: (si, 0)),
            pl.BlockSpec((ts, LANES), lambda bi, si: (si, 0)),
        ],
        out_specs=(head_spec, head_spec,
                   pl.BlockSpec((1, MLA_HEADS, VT_ROWS, ts), lambda bi, si: (bi, 0, 0, si)),
                   *zg_specs),
        scratch_shapes=[pltpu.VMEM((3 * DIL_WIDTH // LANES, ts, LANES), F32)],
        compiler_params=_cparams(("parallel", "parallel")),
        name="even_in_proj",
    )(x, g, wa, wb, gq, gkv, wuq, wuk, wuv, cos_t, sin_t)


def _odd_in_kernel(x_ref, g_ref, wq_ref, wk_ref, wv_ref, gq_ref, gk_ref, cos_ref, sin_ref,
                   q_ref, k_ref, v_ref, *, q_scale):
    ts = x_ref.shape[1]
    xn = _rmsnorm(x_ref[0], g_ref[...]).astype(BF16)
    q_all = _dot(xn, wq_ref[...])
    k_all = _dot(xn, wk_ref[...])
    v_all = _dot(xn, wv_ref[...])
    cos_t = cos_ref[...]
    sin_t = sin_ref[...]
    first = _first_half_mask(ts)
    lane = lax.broadcasted_iota(jnp.int32, (ts, LANES), 1)
    ones_col = (lane == HEAD_DIM).astype(F32)

    def head_norm(t, gain):
        ms = jnp.sum(t * t, axis=-1, keepdims=True) * (1.0 / HEAD_DIM)
        return t * lax.rsqrt(ms + NORM_EPS) * gain

    gq = gq_ref[...] * q_scale
    gk = gk_ref[...]
    for h in range(GQA_HEADS):
        sl = slice(h * LANES, (h + 1) * LANES)
        q_ref[0, h] = _rope_slab(head_norm(q_all[:, sl], gq), cos_t, sin_t, first).astype(BF16)
    for h in range(GQA_KV_HEADS):
        sl = slice(h * LANES, (h + 1) * LANES)
        k_ref[0, h] = (_rope_slab(head_norm(k_all[:, sl], gk), cos_t, sin_t, first) + ones_col).astype(BF16)
        v_ref[0, h] = (v_all[:, sl] + ones_col).T[:VT_ROWS].astype(BF16)


def _odd_in_proj(x, g, wq, wk, wv, gq, gk, cos_t, sin_t, *, ts):
    b, s, d = x.shape
    n_s = s // ts
    q_scale = HEAD_DIM ** -0.5 * LOG2E
    return pl.pallas_call(
        functools.partial(_odd_in_kernel, q_scale=q_scale),
        out_shape=(jax.ShapeDtypeStruct((b, GQA_HEADS, s, LANES), BF16),
                   jax.ShapeDtypeStruct((b, GQA_KV_HEADS, s, LANES), BF16),
                   jax.ShapeDtypeStruct((b, GQA_KV_HEADS, VT_ROWS, s), BF16)),
        grid=(b, n_s),
        in_specs=[
            pl.BlockSpec((1, ts, d), lambda bi, si: (bi, si, 0)),
            _const_spec(g.shape), _const_spec(wq.shape), _const_spec(wk.shape),
            _const_spec(wv.shape), _const_spec(gq.shape), _const_spec(gk.shape),
            pl.BlockSpec((ts, LANES), lambda bi, si: (si, 0)),
            pl.BlockSpec((ts, LANES), lambda bi, si: (si, 0)),
        ],
        out_specs=(pl.BlockSpec((1, GQA_HEADS, ts, LANES), lambda bi, si: (bi, 0, si, 0)),
                   pl.BlockSpec((1, GQA_KV_HEADS, ts, LANES), lambda bi, si: (bi, 0, si, 0)),
                   pl.BlockSpec((1, GQA_KV_HEADS, VT_ROWS, ts), lambda bi, si: (bi, 0, 0, si))),
        compiler_params=_cparams(("parallel", "parallel")),
        name="odd_in_proj",
    )(x, g, wq, wk, wv, gq, gk, cos_t, sin_t)


_NT_DIMS = (((1,), (1,)), ((), ()))


def _attn_kernel(q_ref, k_ref, v_ref, o_ref, kmax_scr, m_scr, acc_scr, p_scr,
                 *, hb, rep, tk, tk_fast, bias_lane, fast_limit):
    tq = q_ref.shape[2]
    s_len = k_ref.shape[2]
    m_rows = rep * tq

    @pl.when(pl.program_id(2) == 0)
    def _():
        ones = jnp.ones((LANES, LANES), BF16)
        for h in range(hb):
            def kbody(j, best, h=h):
                start = pl.multiple_of(j * tk, tk)
                kc = k_ref[0, h, pl.ds(start, tk), :].astype(F32)
                kn2 = _dot((kc * kc).astype(BF16), ones)
                return jnp.maximum(best, jnp.max(kn2, axis=0, keepdims=True))

            best = lax.fori_loop(0, s_len // tk, kbody, jnp.zeros((1, LANES), F32))
            kmax_scr[h] = jnp.broadcast_to(best, (SUBLANES, LANES))

    lane = lax.broadcasted_iota(jnp.int32, (m_rows, LANES), 1)
    outs = []
    for h in range(hb):
        q2 = q_ref[0, h * rep:(h + 1) * rep].reshape(m_rows, LANES)
        qf = q2.astype(F32)
        qn2 = _dot((qf * qf).astype(BF16), jnp.ones((LANES, LANES), BF16))
        qmax2 = jnp.max(qn2, axis=0, keepdims=True)
        bound = jnp.sqrt(qmax2 * kmax_scr[h][0:1, :]) * BOUND_SLACK
        use_fast = jnp.max(bound) <= fast_limit
        acc_scr[...] = jnp.zeros((VT_ROWS, m_rows), F32)

        @pl.when(use_fast)
        def _(h=h, qf=qf, bound=bound):
            q_aug = jnp.where(lane == bias_lane, -bound, qf).astype(BF16)

            def probs(j):
                start = pl.multiple_of(j * tk_fast, tk_fast)
                kc = k_ref[0, h, pl.ds(start, tk_fast), :]
                st = lax.dot_general(kc, q_aug, _NT_DIMS, preferred_element_type=F32)
                return jnp.exp2(st).astype(BF16)

            def accumulate(j, slot):
                start = pl.multiple_of(j * tk_fast, tk_fast)
                acc_scr[...] += _dot(v_ref[0, h, :, pl.ds(start, tk_fast)], p_scr[slot])

            n_chunks = s_len // tk_fast
            p_scr[0] = probs(0)

            def body(i, carry):
                nxt = probs(2 * i + 1)
                accumulate(2 * i, 0)
                p_scr[1] = nxt
                nxt = probs(2 * i + 2)
                accumulate(2 * i + 1, 1)
                p_scr[0] = nxt
                return carry

            lax.fori_loop(0, n_chunks // 2 - 1, body, 0)
            nxt = probs(n_chunks - 1)
            accumulate(n_chunks - 2, 0)
            p_scr[1] = nxt
            accumulate(n_chunks - 1, 1)

        @pl.when(jnp.logical_not(use_fast))
        def _(h=h, q2=q2):
            m_scr[...] = jnp.full((1, m_rows), MASKED, F32)

            def body(j, carry):
                start = pl.multiple_of(j * tk, tk)
                kc = k_ref[0, h, pl.ds(start, tk), :]
                vtc = v_ref[0, h, :, pl.ds(start, tk)]
                st = lax.dot_general(kc, q2, _NT_DIMS, preferred_element_type=F32)
                m_old = m_scr[...]
                m_new = jnp.maximum(m_old, jnp.max(st, axis=0, keepdims=True))
                p = jnp.exp2(st - m_new)
                alpha = jnp.exp2(m_old - m_new)
                acc_scr[...] = alpha * acc_scr[...] + _dot(vtc, p.astype(BF16))
                m_scr[...] = m_new
                return carry

            lax.fori_loop(0, s_len // tk, body, 0)

        acc = acc_scr[...]
        o = (acc[:HEAD_DIM] / acc[HEAD_DIM:HEAD_DIM + 1]).T
        for r in range(rep):
            outs.append(o[r * tq:(r + 1) * tq])
    o_ref[0] = jnp.concatenate(outs, axis=-1).astype(o_ref.dtype)


def _attention(q, k, v, *, hb, tq, tk, tk_fast, bias_lane, fast_limit=FAST_BOUND_LIMIT):
    b, hq, s, _ = q.shape
    hk = k.shape[1]
    rep = hq // hk
    width = hb * rep * HEAD_DIM
    m_rows = rep * tq
    assert (s // tk_fast) % 2 == 0, "the two-slot pipeline consumes key chunks in pairs"
    return pl.pallas_call(
        functools.partial(_attn_kernel, hb=hb, rep=rep, tk=tk, tk_fast=tk_fast,
                          bias_lane=bias_lane, fast_limit=fast_limit),
        out_shape=jax.ShapeDtypeStruct((b, s, hq * HEAD_DIM), BF16),
        grid=(b, hk // hb, s // tq),
        in_specs=[
            pl.BlockSpec((1, hb * rep, tq, LANES), lambda bi, gi, qi: (bi, gi, qi, 0)),
            pl.BlockSpec((1, hb, s, LANES), lambda bi, gi, qi: (bi, gi, 0, 0)),
            pl.BlockSpec((1, hb, VT_ROWS, s), lambda bi, gi, qi: (bi, gi, 0, 0)),
        ],
        out_specs=pl.BlockSpec((1, tq, width), lambda bi, gi, qi: (bi, qi, gi)),
        scratch_shapes=[pltpu.VMEM((hb, SUBLANES, LANES), F32), pltpu.VMEM((1, m_rows), F32),
                        pltpu.VMEM((VT_ROWS, m_rows), F32), pltpu.VMEM((2, tk_fast, m_rows), BF16)],
        compiler_params=_cparams(("parallel", "parallel", "arbitrary")),
        name="dense_attention",
    )(q, k, v)


def _dilated_kernel(q_ref, k_ref, v_ref, o_ref, lse_ref, *, dilation, slopes, qb, win):
    length = k_ref.shape[2]
    u0 = pl.program_id(2) * qb
    start = jnp.clip(u0 - DIL_HALF, 0, length - win)
    start = pl.multiple_of(start, DIL_HALF)
    kw = k_ref[0, 0, pl.ds(start, win), :]
    vw = v_ref[0, 0, pl.ds(start, win), :]
    q = q_ref[0, 0]
    row = lax.broadcasted_iota(jnp.int32, (qb, win), 0)
    col = lax.broadcasted_iota(jnp.int32, (qb, win), 1)
    rel = col - row + (start - u0)
    dist = jnp.abs(rel)
    dist_m = jnp.where(dist <= DIL_HALF, dist.astype(F32), -MASKED)
    lane = lax.broadcasted_iota(jnp.int32, (qb, DIL_WIDTH), 1)
    o_all = jnp.zeros((qb, DIL_WIDTH), F32)
    lse_all = jnp.zeros((qb, DIL_WIDTH), F32)
    for sl in range(DIL_SLOTS):
        in_slot = (lane >= sl * HEAD_DIM) & (lane < (sl + 1) * HEAD_DIM)
        q_s = jnp.where(in_slot, q, jnp.zeros_like(q))
        s = lax.dot_general(q_s, kw, _NT_DIMS, preferred_element_type=F32)
        s = s - (slopes[sl] * dilation * LOG2E) * dist_m
        m = jnp.max(s, axis=-1, keepdims=True)
        e = jnp.exp2(s - m)
        den = jnp.sum(e, axis=-1, keepdims=True)
        o = _dot(e.astype(BF16), vw) / den
        lse = (m + jnp.log2(den)) * LN2
        o_all = jnp.where(in_slot, o, o_all)
        lse_all = jnp.where(in_slot, lse, lse_all)
    o_ref[0, 0] = o_all.astype(o_ref.dtype)
    lse_ref[0, 0] = lse_all


def _dilated_group(zg, group, *, qb_max):
    b, dilation, length, _ = zg.shape
    assert dilation == DIL_PAIRS[group][1]
    qb = min(qb_max, length)
    win = min(length, qb + 2 * DIL_HALF)
    slopes = tuple(2.0 ** (-8.0 * (group * DIL_SLOTS + i + 1) / DIL_HEADS) for i in range(DIL_SLOTS))
    out_spec = pl.BlockSpec((1, 1, qb, DIL_WIDTH), lambda bi, r, j: (bi, r, j, 0))
    return pl.pallas_call(
        functools.partial(_dilated_kernel, dilation=dilation, slopes=slopes, qb=qb, win=win),
        out_shape=(jax.ShapeDtypeStruct((b, dilation, length, DIL_WIDTH), BF16),
                   jax.ShapeDtypeStruct((b, dilation, length, DIL_WIDTH), F32)),
        grid=(b, dilation, length // qb),
        in_specs=[
            pl.BlockSpec((1, 1, qb, DIL_WIDTH), lambda bi, r, j: (bi, r, j, 0)),
            pl.BlockSpec((1, 1, length, DIL_WIDTH), lambda bi, r, j: (bi, r, 0, 1)),
            pl.BlockSpec((1, 1, length, DIL_WIDTH), lambda bi, r, j: (bi, r, 0, 2)),
        ],
        out_specs=(out_spec, out_spec),
        compiler_params=_cparams(("parallel", "parallel", "arbitrary")),
        name=f"dilated_attention_g{group}",
    )(zg, zg, zg)


def _ffn_tail(x1, gf_ref, wg_ref, wu_ref, wd_ref, gfin_ref, o_ref, *, final):
    hn = _rmsnorm(x1, gf_ref[...]).astype(BF16)
    gate = _dot(hn, wg_ref[...])
    up = _dot(hn, wu_ref[...])
    act = gate * (1.0 / (1.0 + jnp.exp(-gate))) * up
    y = x1 + _dot(act.astype(BF16), wd_ref[...])
    if final:
        y = _rmsnorm(y, gfin_ref[...])
    o_ref[0] = y


def _token_major(ref, scr):
    dil, per_class = ref.shape[1], ref.shape[2]
    if dil == 1:
        return ref[0, 0].astype(F32)
    for r in range(dil):
        blk = ref[0, r].astype(F32)
        for c in range(DIL_WIDTH // LANES):
            scr[c, pl.ds(r, per_class, stride=dil), :] = blk[:, c * LANES:(c + 1) * LANES]
    return jnp.concatenate([scr[c] for c in range(DIL_WIDTH // LANES)], axis=-1)


def _even_out_kernel(x_ref, oa_ref, o0_ref, o1_ref, o2_ref, l0_ref, l1_ref, l2_ref,
                     woa_ref, wob_ref, gf_ref, wg_ref, wu_ref, wd_ref, gfin_ref, o_ref,
                     so1_scr, so2_scr, sl1_scr, sl2_scr, *, final):
    o0, o1, o2 = _token_major(o0_ref, None), _token_major(o1_ref, so1_scr), _token_major(o2_ref, so2_scr)
    l0, l1, l2 = _token_major(l0_ref, None), _token_major(l1_ref, sl1_scr), _token_major(l2_ref, sl2_scr)
    mx = jnp.maximum(jnp.maximum(l0, l1), l2)
    e0, e1, e2 = jnp.exp(l0 - mx), jnp.exp(l1 - mx), jnp.exp(l2 - mx)
    tot = e0 + e1 + e2
    comb = (e0 / tot) * o0 + (e1 / tot) * o1 + (e2 / tot) * o2
    x1 = x_ref[0] + (_dot(oa_ref[0], woa_ref[...]) + _dot(comb.astype(BF16), wob_ref[...]))
    _ffn_tail(x1, gf_ref, wg_ref, wu_ref, wd_ref, gfin_ref, o_ref, final=final)


def _odd_out_kernel(x_ref, o_in_ref, wo_ref, gf_ref, wg_ref, wu_ref, wd_ref, gfin_ref, o_ref,
                    *, final):
    x1 = x_ref[0] + _dot(o_in_ref[0], wo_ref[...])
    _ffn_tail(x1, gf_ref, wg_ref, wu_ref, wd_ref, gfin_ref, o_ref, final=final)


def _out_ffn(kernel_fn, x, acts, consts, *, ts, final, scratch=()):
    b, s, d = x.shape

    def row_spec(a):
        if a.ndim == 4:
            dil = a.shape[1]
            return pl.BlockSpec((1, dil, ts // dil, a.shape[-1]), lambda bi, si: (bi, 0, si, 0))
        return pl.BlockSpec((1, ts, a.shape[-1]), lambda bi, si: (bi, si, 0))

    return pl.pallas_call(
        functools.partial(kernel_fn, final=final),
        out_shape=jax.ShapeDtypeStruct((b, s, d), F32),
        grid=(b, s // ts),
        in_specs=[row_spec(x)] + [row_spec(a) for a in acts] + [_const_spec(c.shape) for c in consts],
        out_specs=row_spec(x),
        scratch_shapes=list(scratch),
        compiler_params=_cparams(("parallel", "parallel")),
        name=kernel_fn.__name__.strip("_"),
    )(x, *acts, *consts)


def _rope_angles(pos, dim):
    freqs = ROPE_THETA ** (-jnp.arange(0, dim, 2, dtype=F32) / dim)
    ang = pos.astype(F32)[:, None] * freqs[None, :]
    return jnp.cos(ang), jnp.sin(ang)


def _mla_tables(s):
    cos, sin = _rope_angles(jnp.arange(s), MLA_ROPE)
    one = jnp.ones((s, MLA_NOPE), F32)
    tail = LANES - MLA_NOPE - MLA_ROPE
    cos_t = jnp.concatenate([one, cos, cos, jnp.ones((s, tail), F32)], axis=-1)
    sin_t = jnp.concatenate([jnp.zeros((s, MLA_NOPE), F32), -sin, sin, jnp.zeros((s, tail), F32)], axis=-1)
    return cos_t, sin_t


def _axial_tables(s):
    rows = s // GRID_W
    row_idx = jnp.broadcast_to(jnp.arange(rows)[:, None], (rows, GRID_W)).reshape(-1)
    col_idx = jnp.broadcast_to(jnp.arange(GRID_W)[None, :], (rows, GRID_W)).reshape(-1)
    cos_r, sin_r = _rope_angles(row_idx, HEAD_DIM // 2)
    cos_c, sin_c = _rope_angles(col_idx, HEAD_DIM // 2)
    pad = LANES - HEAD_DIM
    cos_t = jnp.concatenate([cos_r, cos_r, cos_c, cos_c, jnp.ones((s, pad), F32)], axis=-1)
    sin_t = jnp.concatenate([-sin_r, sin_r, -sin_c, sin_c, jnp.zeros((s, pad), F32)], axis=-1)
    return cos_t, sin_t


def _head_slabs(w, fill):
    kdim, heads, d = w.shape
    return jnp.pad(w, ((0, 0), (0, 0), (0, fill - d))).reshape(kdim, heads * fill).astype(BF16)


def _row(v, width=None):
    v = v.reshape(1, -1).astype(F32)
    if width is not None:
        v = jnp.pad(v, ((0, 0), (0, width - v.shape[1])))
    return v


class _Tiles(NamedTuple):
    rows: int
    attn_rows: int
    key_chunk: int
    key_chunk_online: int
    dil_queries: int


def _tiles(s):
    return _Tiles(rows=min(512, s), attn_rows=min(2048, s), key_chunk=min(1024, s // 2),
                  key_chunk_online=min(512, s), dil_queries=256)


def kernel(x, mix_norm_ab, w_in_ab, mla_q_norm, mla_kv_norm, mla_w_uq, mla_w_ukv, w_out_ab,
           mix_norm_c, gqa_w_q, gqa_w_kv, gqa_q_norm, gqa_k_norm, gqa_w_o,
           ffn_norm, ffn_w_in, ffn_w_out, final_norm):
    b, s, d = x.shape
    depth = ffn_norm.shape[0]
    hidden = ffn_w_out.shape[1]
    tiles = _tiles(s)
    ts = tiles.rows
    in_a = MLA_Q_RANK + MLA_KV_RANK
    mla_cos, mla_sin = _mla_tables(s)
    ax_cos, ax_sin = _axial_tables(s)
    gfin = _row(final_norm)

    for layer in range(depth):
        i = layer // 2
        final = layer == depth - 1
        gf = _row(ffn_norm[layer])
        wg = ffn_w_in[layer][:, :hidden].astype(BF16)
        wu = ffn_w_in[layer][:, hidden:].astype(BF16)
        wd = ffn_w_out[layer].astype(BF16)
        if layer % 2 == 0:
            w_in = w_in_ab[i]
            wa = jnp.concatenate([
                w_in[:, :in_a], jnp.zeros((d, MLA_NOPE), F32), w_in[:, in_a:in_a + MLA_ROPE],
                jnp.zeros((d, LANES - MLA_NOPE - MLA_ROPE), F32)], axis=-1).astype(BF16)
            wb = w_in[:, in_a + MLA_ROPE:].astype(BF16)
            wuq = _head_slabs(mla_w_uq[i], LANES)
            wuk = _head_slabs(mla_w_ukv[i][:, :, :MLA_NOPE], LANES)
            wuv = _head_slabs(mla_w_ukv[i][:, :, MLA_NOPE:], LANES)
            q, k, v, *zgs = _even_in_proj(
                x, _row(mix_norm_ab[i]), wa, wb, _row(mla_q_norm[i]), _row(mla_kv_norm[i]),
                wuq, wuk, wuv, mla_cos, mla_sin, ts=ts)
            o_a = _attention(q, k, v, hb=2, tq=tiles.attn_rows, tk=tiles.key_chunk_online,
                             tk_fast=tiles.key_chunk, bias_lane=MLA_BIAS_LANE)
            outs, lses = zip(*[_dilated_group(zgs[g], g, qb_max=tiles.dil_queries)
                               for g in range(DIL_GROUPS)])
            n_a = MLA_HEADS * HEAD_DIM
            x = _out_ffn(_even_out_kernel, x, (o_a,) + outs + lses,
                         (w_out_ab[i][:n_a].astype(BF16), w_out_ab[i][n_a:].astype(BF16),
                          gf, wg, wu, wd, gfin), ts=ts, final=final,
                         scratch=[pltpu.VMEM((DIL_WIDTH // LANES, ts, LANES), F32)] * 4)
        else:
            wq = _head_slabs(gqa_w_q[i].reshape(d, GQA_HEADS, HEAD_DIM), LANES)
            wkv = gqa_w_kv[i].reshape(d, 2, GQA_KV_HEADS, HEAD_DIM)
            wk = _head_slabs(wkv[:, 0], LANES)
            wv = _head_slabs(wkv[:, 1], LANES)
            q, k, v = _odd_in_proj(
                x, _row(mix_norm_c[i]), wq, wk, wv, _row(gqa_q_norm[i], LANES),
                _row(gqa_k_norm[i], LANES), ax_cos, ax_sin, ts=ts)
            rep = GQA_HEADS // GQA_KV_HEADS
            o = _attention(q, k, v, hb=1, tq=max(tiles.attn_rows // rep, min(128, s)),
                           tk=tiles.key_chunk_online, tk_fast=tiles.key_chunk, bias_lane=GQA_BIAS_LANE)
            x = _out_ffn(_odd_out_kernel, x, (o,),
                         (gqa_w_o[i].astype(BF16), gf, wg, wu, wd, gfin), ts=ts, final=final)
    return x
```
